```python
import math
import jax, jax.numpy as jnp
from jax import lax
import numpy as np

D_MODEL = 1024
BATCH = 8
SEQ = 4096
DEPTH = 2

HEAD_DIM = 128
N_Q_HEADS = 8
N_KV_HEADS = 2
GROUP = N_Q_HEADS // N_KV_HEADS
ATTN_WIDTH = N_Q_HEADS * HEAD_DIM
KV_WIDTH = N_KV_HEADS * HEAD_DIM
Q_BLOCK = 128
ROPE_THETA = 10000.0
ROPE_AXIS_DIM = HEAD_DIM // 2
GRID_W = 64
LRU_WIDTH = D_MODEL
LRU_BLOCKS = 8
LRU_BLOCK_W = LRU_WIDTH // LRU_BLOCKS
LRU_C = 8.0
CONV_WIDTH = 4
CONV_PAD_LEFT = 2
N_DIR = 2
D_FF = 2816
FFN_RES = 0.5
N_MOD = 9
EPS = 1e-6
SPLITS = [ATTN_WIDTH,
          ATTN_WIDTH + KV_WIDTH,
          ATTN_WIDTH + 2 * KV_WIDTH,
          ATTN_WIDTH + 2 * KV_WIDTH + LRU_WIDTH,
          ATTN_WIDTH + 2 * KV_WIDTH + 2 * LRU_WIDTH]
IN_COLS = ATTN_WIDTH + 2 * KV_WIDTH + 2 * LRU_WIDTH + 2 * D_MODEL

kernel_name = "hybrid_rglru_axial_gqa_macaron_encoder"


def _rmsnorm(x, g):
    xf = x.astype(jnp.float32)
    y = xf * lax.rsqrt(jnp.mean(xf * xf, axis=-1, keepdims=True) + EPS) * g.astype(jnp.float32)
    return y.astype(x.dtype)


def _modulate(h, shift, scale):
    return h * (1 + scale[:, None, :]) + shift[:, None, :]


def _swiglu(h, w_up, w_down):
    gate, up = jnp.split(h @ w_up, 2, axis=-1)
    return (jax.nn.silu(gate) * up) @ w_down


def _axial_rope_tables(S):
    rows = S // GRID_W
    row_ids = jnp.broadcast_to(jnp.arange(rows, dtype=jnp.float32)[:, None], (rows, GRID_W)).reshape(S)
    col_ids = jnp.broadcast_to(jnp.arange(GRID_W, dtype=jnp.float32)[None, :], (rows, GRID_W)).reshape(S)
    inv_freq = ROPE_THETA ** (-jnp.arange(0, ROPE_AXIS_DIM, 2, dtype=jnp.float32) / ROPE_AXIS_DIM)
    ang = jnp.concatenate([row_ids[:, None] * inv_freq, col_ids[:, None] * inv_freq], axis=-1)
    return jnp.cos(ang), jnp.sin(ang)


def _apply_rope(x, cos, sin):
    B, S, H, Dh = x.shape
    xp = x.astype(jnp.float32).reshape(B, S, H, Dh // 2, 2)
    x0, x1 = xp[..., 0], xp[..., 1]
    c = cos[None, :, None, :]
    s = sin[None, :, None, :]
    out = jnp.stack([x0 * c - x1 * s, x0 * s + x1 * c], axis=-1).reshape(B, S, H, Dh)
    return out.astype(x.dtype)


def _grid_attention(q, k, v, q_g, k_g):
    B, S = q.shape[0], q.shape[1]
    cos, sin = _axial_rope_tables(S)
    q = _apply_rope(_rmsnorm(q, q_g), cos, sin)
    k = _apply_rope(_rmsnorm(k, k_g), cos, sin)
    n_blk = S // Q_BLOCK
    qb = q.reshape(B, n_blk, Q_BLOCK, N_KV_HEADS, GROUP, HEAD_DIM).transpose(1, 0, 2, 3, 4, 5)
    scale = HEAD_DIM ** -0.5

    def one_block(qblk):
        s = jnp.einsum('bqkgd,bskd->bkgqs', qblk, k).astype(jnp.float32) * scale
        p = jax.nn.softmax(s, axis=-1).astype(v.dtype)
        return jnp.einsum('bkgqs,bskd->bqkgd', p, v)

    o = lax.map(one_block, qb)
    return o.transpose(1, 0, 2, 3, 4, 5).reshape(B, S, ATTN_WIDTH)


def _centred_dwconv(x, w, b):
    rhs = w[:, None, :].astype(x.dtype)
    y = lax.conv_general_dilated(x, rhs, window_strides=(1,),
                                 padding=[(CONV_PAD_LEFT, CONV_WIDTH - 1 - CONV_PAD_LEFT)],
                                 dimension_numbers=('NWC', 'WIO', 'NWC'),
                                 feature_group_count=LRU_WIDTH)
    return y + b


def _rg_lru(x, w_a, b_a, w_x, b_x, lam, reverse):
    B, S, W = x.shape
    xb = x.reshape(B, S, LRU_BLOCKS, LRU_BLOCK_W)
    r = jax.nn.sigmoid(jnp.einsum('bshi,hij->bshj', xb, w_a.astype(jnp.float32)) + b_a).reshape(B, S, W)
    i = jax.nn.sigmoid(jnp.einsum('bshi,hij->bshj', xb, w_x.astype(jnp.float32)) + b_x).reshape(B, S, W)
    log_a = -LRU_C * r * jax.nn.softplus(-lam.astype(jnp.float32))
    a = jnp.exp(log_a)
    u = jnp.sqrt(-jnp.expm1(2.0 * log_a)) * (i * x)

    def combine(e1, e2):
        a1, b1 = e1
        a2, b2 = e2
        return a1 * a2, a2 * b1 + b2

    _, h = lax.associative_scan(combine, (a, u), axis=1, reverse=reverse)
    return h


def _token_mixers(h, w_in, q_g, k_g, conv_w, conv_b, wa, ba, wx, bx, lam, w_attn_o, w_lru_o, w_out):
    B, S, _ = h.shape
    proj = h @ w_in
    q, k, v, lx, lg, gates = jnp.split(proj, SPLITS, axis=-1)
    attn = _grid_attention(q.reshape(B, S, N_Q_HEADS, HEAD_DIM),
                           k.reshape(B, S, N_KV_HEADS, HEAD_DIM),
                           v.reshape(B, S, N_KV_HEADS, HEAD_DIM), q_g, k_g)
    xc = _centred_dwconv(lx, conv_w, conv_b).astype(jnp.float32)
    h_lru = (_rg_lru(xc, wa[0], ba[0], wx[0], bx[0], lam[0], False)
             + _rg_lru(xc, wa[1], ba[1], wx[1], bx[1], lam[1], True))
    lru = h_lru.astype(h.dtype) * jax.nn.gelu(lg)
    g_attn, g_lru = jnp.split(jax.nn.sigmoid(gates), 2, axis=-1)
    merged = g_attn * (attn @ w_attn_o) + g_lru * (lru @ w_lru_o)
    return merged @ w_out


def setup_inputs(seed: int = 0) -> dict:
    key = jax.random.key(seed)
    ks = jax.random.split(key, 24)
    f32 = jnp.float32

    def nrm(k, shape, scale):
        return jax.random.normal(k, shape, f32) * scale

    a0 = jax.random.uniform(ks[17], (DEPTH, N_DIR, LRU_WIDTH), f32, 0.9, 0.999)
    return {
        "x": nrm(ks[0], (BATCH, SEQ, D_MODEL), 1.0),
        "c": nrm(ks[1], (BATCH, D_MODEL), 1.0),
        "ada_w": nrm(ks[2], (DEPTH, D_MODEL, N_MOD * D_MODEL), 0.5 * D_MODEL ** -0.5),
        "ada_b": nrm(ks[3], (DEPTH, N_MOD * D_MODEL), 0.02),
        "norm_g": 1.0 + nrm(ks[4], (DEPTH, 3, D_MODEL), 0.05),
        "ffn1_up": nrm(ks[5], (DEPTH, D_MODEL, 2 * D_FF), D_MODEL ** -0.5),
        "ffn1_down": nrm(ks[6], (DEPTH, D_FF, D_MODEL), D_FF ** -0.5),
        "w_in": nrm(ks[7], (DEPTH, D_MODEL, IN_COLS), D_MODEL ** -0.5),
        "q_norm_g": 1.0 + nrm(ks[8], (DEPTH, HEAD_DIM), 0.05),
        "k_norm_g": 1.0 + nrm(ks[9], (DEPTH, HEAD_DIM), 0.05),
        "conv_w": nrm(ks[10], (DEPTH, CONV_WIDTH, LRU_WIDTH), CONV_WIDTH ** -0.5),
        "conv_b": nrm(ks[11], (DEPTH, LRU_WIDTH), 0.02),
        "lru_wa": nrm(ks[12], (DEPTH, N_DIR, LRU_BLOCKS, LRU_BLOCK_W, LRU_BLOCK_W), LRU_BLOCK_W ** -0.5),
        "lru_ba": nrm(ks[13], (DEPTH, N_DIR, LRU_BLOCKS, LRU_BLOCK_W), 0.1),
        "lru_wx": nrm(ks[14], (DEPTH, N_DIR, LRU_BLOCKS, LRU_BLOCK_W, LRU_BLOCK_W), LRU_BLOCK_W ** -0.5),
        "lru_bx": nrm(ks[15], (DEPTH, N_DIR, LRU_BLOCKS, LRU_BLOCK_W), 0.1),
        "lru_lambda": jnp.log(a0) - jnp.log1p(-a0),
        "w_attn_o": nrm(ks[16], (DEPTH, ATTN_WIDTH, D_MODEL), ATTN_WIDTH ** -0.5),
        "w_lru_o": nrm(ks[18], (DEPTH, LRU_WIDTH, D_MODEL), LRU_WIDTH ** -0.5),
        "w_out": nrm(ks[19], (DEPTH, D_MODEL, D_MODEL), D_MODEL ** -0.5),
        "ffn2_up": nrm(ks[20], (DEPTH, D_MODEL, 2 * D_FF), D_MODEL ** -0.5),
        "ffn2_down": nrm(ks[21], (DEPTH, D_FF, D_MODEL), D_FF ** -0.5),
        "final_g": 1.0 + nrm(ks[22], (D_MODEL,), 0.05),
    }


def reference(x, c, ada_w, ada_b, norm_g, ffn1_up, ffn1_down, w_in, q_norm_g, k_norm_g,
              conv_w, conv_b, lru_wa, lru_ba, lru_wx, lru_bx, lru_lambda,
              w_attn_o, w_lru_o, w_out, ffn2_up, ffn2_down, final_g):
    B = x.shape[0]
    c_act = jax.nn.silu(c)
    for l in range(DEPTH):
        mod = (c_act @ ada_w[l] + ada_b[l]).reshape(B, N_MOD, D_MODEL)
        h = _modulate(_rmsnorm(x, norm_g[l, 0]), mod[:, 0], mod[:, 1])
        x = x + FFN_RES * mod[:, 2][:, None, :] * _swiglu(h, ffn1_up[l], ffn1_down[l])
        h = _modulate(_rmsnorm(x, norm_g[l, 1]), mod[:, 3], mod[:, 4])
        y = _token_mixers(h, w_in[l], q_norm_g[l], k_norm_g[l], conv_w[l], conv_b[l],
                          lru_wa[l], lru_ba[l], lru_wx[l], lru_bx[l], lru_lambda[l],
                          w_attn_o[l], w_lru_o[l], w_out[l])
        x = x + mod[:, 5][:, None, :] * y
        h = _modulate(_rmsnorm(x, norm_g[l, 2]), mod[:, 6], mod[:, 7])
        x = x + FFN_RES * mod[:, 8][:, None, :] * _swiglu(h, ffn2_up[l], ffn2_down[l])
    return _rmsnorm(x, final_g)
```

```python
import functools
import math

import jax
import jax.numpy as jnp
from jax import lax
from jax.experimental import pallas as pl
from jax.experimental.pallas import tpu as pltpu

F32 = jnp.float32
BF16 = jnp.bfloat16

LANES = 128
SUBLANES = 8
VMEM_LIMIT_BYTES = 56 * 1024 * 1024

D_MODEL = 1024
HEAD_DIM = 128
N_Q_HEADS = 8
N_KV_HEADS = 2
GROUP = N_Q_HEADS // N_KV_HEADS
ATTN_WIDTH = N_Q_HEADS * HEAD_DIM
KV_WIDTH = N_KV_HEADS * HEAD_DIM
ROPE_THETA = 10000.0
GRID_W = 64
LRU_WIDTH = D_MODEL
LRU_BLOCKS = 8
LRU_BLOCK_W = LRU_WIDTH // LRU_BLOCKS
LRU_C = 8.0
D_FF = 2816
FFN_RES = 0.5
N_MOD = 9
EPS = 1e-6

OFF_Q = 0
OFF_K = ATTN_WIDTH
OFF_V = OFF_K + KV_WIDTH
OFF_LX = OFF_V + KV_WIDTH
OFF_LG = OFF_LX + LRU_WIDTH
OFF_GATES = OFF_LG + LRU_WIDTH
IN_COLS = OFF_GATES + 2 * D_MODEL

ADA_TN = 2304
FFN_TM = 1024
FFN_TF = 512
FFN_CHUNKS = tuple((lo, min(lo + FFN_TF, D_FF)) for lo in range(0, D_FF, FFN_TF))
PROJ_TM = 512
ATT_TQ = 1024
ATT_KC = 1024
LRU_TC = 128
LRU_HALO = SUBLANES
LRU_PITCH = LRU_TC + SUBLANES
MERGE_TM = 512


def _cparams(semantics):
    return pltpu.CompilerParams(dimension_semantics=semantics, vmem_limit_bytes=VMEM_LIMIT_BYTES)


def _sigmoid(x):
    return 1.0 / (1.0 + jnp.exp(-x))


def _rms_modulate(x, g, shift, scale):
    y = x * lax.rsqrt(jnp.mean(x * x, axis=-1, keepdims=True) + EPS) * g
    return y * (1.0 + scale) + shift


def _ada_kernel(c_ref, w_ref, b_ref, o_ref):
    c = c_ref[...]
    c_act = (c * _sigmoid(c)).astype(BF16)
    o_ref[0] = jnp.dot(c_act, w_ref[0].astype(BF16), preferred_element_type=F32) + b_ref[0]


def _ada_call(c, ada_w, ada_b):
    depth, d, n = ada_w.shape
    b = c.shape[0]
    return pl.pallas_call(
        _ada_kernel,
        grid=(depth, n // ADA_TN),
        in_specs=[
            pl.BlockSpec((b, d), lambda l, j: (0, 0)),
            pl.BlockSpec((1, d, ADA_TN), lambda l, j: (l, 0, j)),
            pl.BlockSpec((1, 1, ADA_TN), lambda l, j: (l, 0, j)),
        ],
        out_specs=pl.BlockSpec((1, b, ADA_TN), lambda l, j: (l, 0, j)),
        out_shape=jax.ShapeDtypeStruct((depth, b, n), F32),
        compiler_params=_cparams(("arbitrary", "arbitrary")),
        name="ada_mod",
    )(c, ada_w, ada_b.reshape(depth, 1, n))


def _ffn_kernel(*refs, mod_base, final_norm):
    if final_norm:
        x_ref, mod_ref, g_ref, wup_ref, wd_ref, fg_ref, o_ref, acc_s = refs
    else:
        x_ref, mod_ref, g_ref, wup_ref, wd_ref, o_ref, acc_s = refs
    shift = mod_ref[0, mod_base:mod_base + 1, :]
    scale = mod_ref[0, mod_base + 1:mod_base + 2, :]
    h = _rms_modulate(x_ref[0], g_ref[...], shift, scale).astype(BF16)
    for n, (lo, hi) in enumerate(FFN_CHUNKS):
        gate = jnp.dot(h, wup_ref[:, lo:hi], preferred_element_type=F32)
        up = jnp.dot(h, wup_ref[:, D_FF + lo:D_FF + hi], preferred_element_type=F32)
        act = (gate * _sigmoid(gate) * up).astype(BF16)
        part = jnp.dot(act, wd_ref[lo:hi, :], preferred_element_type=F32)
        if n == 0:
            acc_s[...] = part
        else:
            acc_s[...] += part
    res_gate = mod_ref[0, mod_base + 2:mod_base + 3, :]
    out = x_ref[0] + (FFN_RES * res_gate) * acc_s[...]
    if final_norm:
        out = out * lax.rsqrt(jnp.mean(out * out, axis=-1, keepdims=True) + EPS) * fg_ref[...]
    o_ref[0] = out


def _ffn_call(x, mod, g, w_up, w_down, mod_base, final_g=None):
    b, s, d = x.shape
    final_norm = final_g is not None
    const2 = lambda bi, i: (0, 0)
    in_specs = [
        pl.BlockSpec((1, FFN_TM, d), lambda bi, i: (bi, i, 0)),
        pl.BlockSpec((1, N_MOD, d), lambda bi, i: (bi, 0, 0)),
        pl.BlockSpec((1, d), const2),
        pl.BlockSpec(w_up.shape, const2, pipeline_mode=pl.Buffered(1)),
        pl.BlockSpec(w_down.shape, const2, pipeline_mode=pl.Buffered(1)),
    ]
    args = [x, mod, g.reshape(1, d), w_up, w_down]
    if final_norm:
        in_specs.append(pl.BlockSpec((1, d), const2))
        args.append(final_g.reshape(1, d))
    return pl.pallas_call(
        functools.partial(_ffn_kernel, mod_base=mod_base, final_norm=final_norm),
        grid=(b, s // FFN_TM),
        in_specs=in_specs,
        out_specs=pl.BlockSpec((1, FFN_TM, d), lambda bi, i: (bi, i, 0)),
        out_shape=jax.ShapeDtypeStruct((b, s, d), F32),
        scratch_shapes=[pltpu.VMEM((FFN_TM, d), F32)],
        compiler_params=_cparams(("arbitrary", "arbitrary")),
        name="ffn_final" if final_norm else "ffn",
    )(*args)


def _head_norm_rope(xh, g, cos, sin):
    y = xh * lax.rsqrt(jnp.mean(xh * xh, axis=-1, keepdims=True) + EPS) * g
    return y * cos + pltpu.roll(y, HEAD_DIM // 2, 1) * sin


def _proj_kernel(x_ref, mod_ref, g_ref, w_ref, qg_ref, kg_ref, cos_ref, sin_ref,
                 q_ref, k_ref, v_ref, lx_ref, lg_ref, gates_ref):
    h = _rms_modulate(x_ref[0], g_ref[...], mod_ref[0, 3:4, :], mod_ref[0, 4:5, :]).astype(BF16)
    cos = cos_ref[...]
    sin = sin_ref[...]

    q = jnp.dot(h, w_ref[:, OFF_Q:OFF_K], preferred_element_type=F32)
    for hd in range(N_Q_HEADS):
        qh = q[:, hd * HEAD_DIM:(hd + 1) * HEAD_DIM]
        q_ref[0, hd] = _head_norm_rope(qh, qg_ref[...], cos, sin).astype(BF16)

    kv = jnp.dot(h, w_ref[:, OFF_K:OFF_LX], preferred_element_type=F32)
    for hd in range(N_KV_HEADS):
        kh = kv[:, hd * HEAD_DIM:(hd + 1) * HEAD_DIM]
        k_ref[0, hd] = _head_norm_rope(kh, kg_ref[...], cos, sin).astype(BF16)
        v_ref[0, hd] = kv[:, KV_WIDTH + hd * HEAD_DIM:KV_WIDTH + (hd + 1) * HEAD_DIM].astype(BF16)

    lx_ref[0] = jnp.dot(h, w_ref[:, OFF_LX:OFF_LG], preferred_element_type=F32)
    lg_ref[0] = jnp.dot(h, w_ref[:, OFF_LG:OFF_GATES], preferred_element_type=F32)
    gates_ref[0] = jnp.dot(h, w_ref[:, OFF_GATES:IN_COLS], preferred_element_type=F32)


def _proj_call(x, mod, g, w_in, q_g, k_g, cos, sin):
    b, s, d = x.shape
    tm = PROJ_TM
    const2 = lambda bi, i: (0, 0)
    return pl.pallas_call(
        _proj_kernel,
        grid=(b, s // tm),
        in_specs=[
            pl.BlockSpec((1, tm, d), lambda bi, i: (bi, i, 0)),
            pl.BlockSpec((1, N_MOD, d), lambda bi, i: (bi, 0, 0)),
            pl.BlockSpec((1, d), const2),
            pl.BlockSpec((d, IN_COLS), const2, pipeline_mode=pl.Buffered(1)),
            pl.BlockSpec((1, HEAD_DIM), const2),
            pl.BlockSpec((1, HEAD_DIM), const2),
            pl.BlockSpec((tm, HEAD_DIM), lambda bi, i: (i, 0)),
            pl.BlockSpec((tm, HEAD_DIM), lambda bi, i: (i, 0)),
        ],
        out_specs=[
            pl.BlockSpec((1, N_Q_HEADS, tm, HEAD_DIM), lambda bi, i: (bi, 0, i, 0)),
            pl.BlockSpec((1, N_KV_HEADS, tm, HEAD_DIM), lambda bi, i: (bi, 0, i, 0)),
            pl.BlockSpec((1, N_KV_HEADS, tm, HEAD_DIM), lambda bi, i: (bi, 0, i, 0)),
            pl.BlockSpec((1, tm, LRU_WIDTH), lambda bi, i: (bi, i, 0)),
            pl.BlockSpec((1, tm, LRU_WIDTH), lambda bi, i: (bi, i, 0)),
            pl.BlockSpec((1, tm, 2 * d), lambda bi, i: (bi, i, 0)),
        ],
        out_shape=[
            jax.ShapeDtypeStruct((b, N_Q_HEADS, s, HEAD_DIM), BF16),
            jax.ShapeDtypeStruct((b, N_KV_HEADS, s, HEAD_DIM), BF16),
            jax.ShapeDtypeStruct((b, N_KV_HEADS, s, HEAD_DIM), BF16),
            jax.ShapeDtypeStruct((b, s, LRU_WIDTH), F32),
            jax.ShapeDtypeStruct((b, s, LRU_WIDTH), F32),
            jax.ShapeDtypeStruct((b, s, 2 * d), F32),
        ],
        compiler_params=_cparams(("arbitrary", "arbitrary")),
        name="in_proj",
    )(x, mod, g.reshape(1, d), w_in, q_g.reshape(1, HEAD_DIM), k_g.reshape(1, HEAD_DIM), cos, sin)


def _attn_kernel(q_ref, k_ref, v_ref, o_ref, s_scr):
    n_kc = k_ref.shape[2] // ATT_KC
    exp2_scale = (HEAD_DIM ** -0.5) * math.log2(math.e)

    def head_body(g, carry):
        q = q_ref[0, g]
        m = jnp.full((ATT_TQ, 1), -jnp.inf, F32)
        for c in range(n_kc):
            kc = k_ref[0, 0, c * ATT_KC:(c + 1) * ATT_KC, :]
            s = lax.dot_general(q, kc, (((1,), (1,)), ((), ())), preferred_element_type=F32)
            s_scr[:, c * ATT_KC:(c + 1) * ATT_KC] = s
            m = jnp.maximum(m, jnp.max(s, axis=-1, keepdims=True))
        l = jnp.zeros((ATT_TQ, 1), F32)
        acc = jnp.zeros((ATT_TQ, HEAD_DIM), F32)
        for c in range(n_kc):
            p = jnp.exp2((s_scr[:, c * ATT_KC:(c + 1) * ATT_KC] - m) * exp2_scale)
            l = l + jnp.sum(p, axis=-1, keepdims=True)
            vc = v_ref[0, 0, c * ATT_KC:(c + 1) * ATT_KC, :]
            acc = acc + jnp.dot(p.astype(BF16), vc, preferred_element_type=F32)
        o_ref[0, g] = (acc / l).astype(BF16)
        return carry

    lax.fori_loop(0, GROUP, head_body, 0)


def _attn_call(q, k, v):
    b, _, s, _ = q.shape
    return pl.pallas_call(
        _attn_kernel,
        grid=(b, N_KV_HEADS, s // ATT_TQ),
        in_specs=[
            pl.BlockSpec((1, GROUP, ATT_TQ, HEAD_DIM), lambda bi, kh, i: (bi, kh, i, 0)),
            pl.BlockSpec((1, 1, s, HEAD_DIM), lambda bi, kh, i: (bi, kh, 0, 0)),
            pl.BlockSpec((1, 1, s, HEAD_DIM), lambda bi, kh, i: (bi, kh, 0, 0)),
        ],
        out_specs=pl.BlockSpec((1, GROUP, ATT_TQ, HEAD_DIM), lambda bi, kh, i: (bi, kh, i, 0)),
        out_shape=jax.ShapeDtypeStruct((b, N_Q_HEADS, s, HEAD_DIM), BF16),
        scratch_shapes=[pltpu.VMEM((ATT_TQ, s), F32)],
        compiler_params=_cparams(("arbitrary", "arbitrary", "arbitrary")),
        name="attention",
    )(q, k, v)


def _gelu_tanh(x):
    return 0.5 * x * (1.0 + jnp.tanh(math.sqrt(2.0 / math.pi) * (x + 0.044715 * (x * x * x))))


def _lru_kernel(*refs, reverse):
    if reverse:
        (x_ref, prev_ref, next_ref, cw_ref, cb_ref, wg_ref, ba_ref, bx_ref, lam_ref, hf_ref, lg_ref,
         o_ref, xe_s, xc_s, a_s, u_s, hc_s) = refs
    else:
        (x_ref, prev_ref, next_ref, cw_ref, cb_ref, wg_ref, ba_ref, bx_ref, lam_ref,
         o_ref, xe_s, xc_s, a_s, u_s, hc_s) = refs
    nb = x_ref.shape[0]
    tc, pitch, halo = LRU_TC, LRU_PITCH, LRU_HALO
    i = pl.program_id(0)
    n = pl.num_programs(0)
    chunk = (n - 1 - i) if reverse else i

    @pl.when(i == 0)
    def _():
        hc_s[...] = jnp.zeros_like(hc_s)

    xe_s[:, 0:halo, :] = jnp.where(chunk == 0, 0.0, prev_ref[...])
    xe_s[:, halo:halo + tc, :] = x_ref[...]
    xe_s[:, halo + tc:2 * halo + tc, :] = jnp.where(chunk == n - 1, 0.0, next_ref[...])
    for b in range(nb):
        xc = cb_ref[...] + xe_s[b, halo - 2:halo - 2 + tc, :] * cw_ref[0:1, :]
        xc = xc + xe_s[b, halo - 1:halo - 1 + tc, :] * cw_ref[1:2, :]
        xc = xc + xe_s[b, halo:halo + tc, :] * cw_ref[2:3, :]
        xc = xc + xe_s[b, halo + 1:halo + 1 + tc, :] * cw_ref[3:4, :]
        xc_s[b * tc:(b + 1) * tc, :] = xc

    for k in range(LRU_BLOCKS):
        cols = slice(k * LRU_BLOCK_W, (k + 1) * LRU_BLOCK_W)
        xb = xc_s[:, cols]
        pre = jnp.dot(xb.astype(BF16), wg_ref[k], preferred_element_type=F32)
        r = _sigmoid(pre[:, :LRU_BLOCK_W] + ba_ref[:, cols])
        ig = _sigmoid(pre[:, LRU_BLOCK_W:] + bx_ref[:, cols])
        z = -lam_ref[:, cols]
        softplus = jnp.maximum(z, 0.0) + jnp.log1p(jnp.exp(-jnp.abs(z)))
        a = jnp.exp(-LRU_C * r * softplus)
        u = jnp.sqrt(1.0 - a * a) * (ig * xb)
        for b in range(nb):
            a_s[k, b * pitch:b * pitch + tc, :] = a[b * tc:(b + 1) * tc]
            u_s[k, b * pitch:b * pitch + tc, :] = u[b * tc:(b + 1) * tc]

    def step(t, hs):
        tt = (tc - 1 - t) if reverse else t
        new = []
        for k in range(LRU_BLOCKS):
            rows = pl.ds(tt, nb, stride=pitch)
            hk = a_s[k, rows, :] * hs[k] + u_s[k, rows, :]
            u_s[k, rows, :] = hk
            new.append(hk)
        return tuple(new)

    hs = lax.fori_loop(0, tc, step, tuple(hc_s[k] for k in range(LRU_BLOCKS)), unroll=4)
    for k in range(LRU_BLOCKS):
        hc_s[k] = hs[k]

    for k in range(LRU_BLOCKS):
        cols = slice(k * LRU_BLOCK_W, (k + 1) * LRU_BLOCK_W)
        for b in range(nb):
            hk = u_s[k, b * pitch:b * pitch + tc, :]
            if reverse:
                o_ref[b, :, cols] = ((hf_ref[b, :, cols] + hk) * _gelu_tanh(lg_ref[b, :, cols])).astype(BF16)
            else:
                o_ref[b, :, cols] = hk


def _lru_call(lx, conv_w, conv_b, w_gate, b_a, b_x, lam, reverse, hf=None, lg=None):
    b, s, w = lx.shape
    tc, halo = LRU_TC, LRU_HALO
    n = s // tc
    hb_per_chunk = tc // halo
    last_hb = s // halo - 1
    cidx = (lambda i: n - 1 - i) if reverse else (lambda i: i)
    const2 = lambda i: (0, 0)
    chunk_spec = pl.BlockSpec((b, tc, w), lambda i: (0, cidx(i), 0))
    in_specs = [
        chunk_spec,
        pl.BlockSpec((b, halo, w), lambda i: (0, jnp.maximum(cidx(i) * hb_per_chunk - 1, 0), 0)),
        pl.BlockSpec((b, halo, w), lambda i: (0, jnp.minimum((cidx(i) + 1) * hb_per_chunk, last_hb), 0)),
        pl.BlockSpec(conv_w.shape, const2),
        pl.BlockSpec((1, w), const2),
        pl.BlockSpec(w_gate.shape, lambda i: (0, 0, 0)),
        pl.BlockSpec((1, w), const2),
        pl.BlockSpec((1, w), const2),
        pl.BlockSpec((1, w), const2),
    ]
    args = [lx, lx, lx, conv_w, conv_b.reshape(1, w), w_gate, b_a.reshape(1, w), b_x.reshape(1, w),
            lam.reshape(1, w)]
    if reverse:
        in_specs += [chunk_spec, chunk_spec]
        args += [hf, lg]
    return pl.pallas_call(
        functools.partial(_lru_kernel, reverse=reverse),
        grid=(n,),
        in_specs=in_specs,
        out_specs=chunk_spec,
        out_shape=jax.ShapeDtypeStruct((b, s, w), BF16 if reverse else F32),
        scratch_shapes=[
            pltpu.VMEM((b, tc + 2 * halo, w), F32),
            pltpu.VMEM((b * tc, w), F32),
            pltpu.VMEM((LRU_BLOCKS, b * LRU_PITCH, LRU_BLOCK_W), F32),
            pltpu.VMEM((LRU_BLOCKS, b * LRU_PITCH, LRU_BLOCK_W), F32),
            pltpu.VMEM((LRU_BLOCKS, b, LRU_BLOCK_W), F32),
        ],
        compiler_params=_cparams(("arbitrary",)),
        name="lru_bwd" if reverse else "lru_fwd",
    )(*args)


def _merge_kernel(attn_ref, lru_ref, gates_ref, x_ref, mod_ref, wa_ref, wl_ref, wo_ref, o_ref, acat_s):
    for hd in range(N_Q_HEADS):
        acat_s[:, hd * HEAD_DIM:(hd + 1) * HEAD_DIM] = attn_ref[0, hd]
    a = jnp.dot(acat_s[...], wa_ref[...], preferred_element_type=F32)
    r = jnp.dot(lru_ref[0], wl_ref[...], preferred_element_type=F32)
    g = _sigmoid(gates_ref[0])
    merged = g[:, :D_MODEL] * a + g[:, D_MODEL:] * r
    y = jnp.dot(merged.astype(BF16), wo_ref[...], preferred_element_type=F32)
    o_ref[0] = x_ref[0] + mod_ref[0, 5:6, :] * y


def _merge_call(attn, lru, gates, x, mod, w_attn_o, w_lru_o, w_out):
    b, s, d = x.shape
    tm = MERGE_TM
    const2 = lambda bi, i: (0, 0)
    tok = lambda bi, i: (bi, i, 0)
    return pl.pallas_call(
        _merge_kernel,
        grid=(b, s // tm),
        in_specs=[
            pl.BlockSpec((1, N_Q_HEADS, tm, HEAD_DIM), lambda bi, i: (bi, 0, i, 0)),
            pl.BlockSpec((1, tm, LRU_WIDTH), tok),
            pl.BlockSpec((1, tm, 2 * d), tok),
            pl.BlockSpec((1, tm, d), tok),
            pl.BlockSpec((1, N_MOD, d), lambda bi, i: (bi, 0, 0)),
            pl.BlockSpec((ATTN_WIDTH, d), const2),
            pl.BlockSpec((LRU_WIDTH, d), const2),
            pl.BlockSpec((d, d), const2),
        ],
        out_specs=pl.BlockSpec((1, tm, d), tok),
        out_shape=jax.ShapeDtypeStruct((b, s, d), F32),
        scratch_shapes=[pltpu.VMEM((tm, ATTN_WIDTH), BF16)],
        compiler_params=_cparams(("arbitrary", "arbitrary")),
        name="merge_out",
    )(attn, lru, gates, x, mod, w_attn_o, w_lru_o, w_out)


def _rope_tables(s):
    rows = s // GRID_W
    row_ids = jnp.broadcast_to(jnp.arange(rows, dtype=F32)[:, None], (rows, GRID_W)).reshape(s)
    col_ids = jnp.broadcast_to(jnp.arange(GRID_W, dtype=F32)[None, :], (rows, GRID_W)).reshape(s)
    axis_dim = HEAD_DIM // 2
    inv_freq = ROPE_THETA ** (-jnp.arange(0, axis_dim, 2, dtype=F32) / axis_dim)
    ang = jnp.concatenate([row_ids[:, None] * inv_freq, col_ids[:, None] * inv_freq], axis=-1)
    cos, sin = jnp.cos(ang), jnp.sin(ang)
    return jnp.concatenate([cos, cos], axis=-1), jnp.concatenate([-sin, sin], axis=-1)


def _deinterleave_heads(w):
    lead = w.shape[:-1]
    n_heads = w.shape[-1] // HEAD_DIM
    w = w.reshape(*lead, n_heads, HEAD_DIM // 2, 2)
    return jnp.swapaxes(w, -1, -2).reshape(*lead, n_heads * HEAD_DIM)


def kernel(x, c, ada_w, ada_b, norm_g, ffn1_up, ffn1_down, w_in, q_norm_g, k_norm_g, conv_w, conv_b,
           lru_wa, lru_ba, lru_wx, lru_bx, lru_lambda, w_attn_o, w_lru_o, w_out, ffn2_up, ffn2_down, final_g):
    b, s, d = x.shape
    depth = ada_w.shape[0]
    mod = _ada_call(c, ada_w, ada_b).reshape(depth, b, N_MOD, d)
    cos, sin = _rope_tables(s)

    for l in range(depth):
        w_in_l = jnp.concatenate([_deinterleave_heads(w_in[l][:, :OFF_V]), w_in[l][:, OFF_V:]],
                                 axis=-1).astype(BF16)
        x = _ffn_call(x, mod[l], norm_g[l, 0], ffn1_up[l].astype(BF16), ffn1_down[l].astype(BF16), 0)

        q, k, v, lx, lg, gates = _proj_call(x, mod[l], norm_g[l, 1], w_in_l, _deinterleave_heads(q_norm_g[l]),
                                            _deinterleave_heads(k_norm_g[l]), cos, sin)
        attn = _attn_call(q, k, v)
        w_gate = jnp.concatenate([lru_wa[l], lru_wx[l]], axis=-1).astype(BF16)
        hf = _lru_call(lx, conv_w[l], conv_b[l], w_gate[0], lru_ba[l, 0], lru_bx[l, 0], lru_lambda[l, 0], False)
        lru = _lru_call(lx, conv_w[l], conv_b[l], w_gate[1], lru_ba[l, 1], lru_bx[l, 1], lru_lambda[l, 1], True,
                        hf=hf, lg=lg)
        x = _merge_call(attn, lru, gates, x, mod[l], w_attn_o[l].astype(BF16), w_lru_o[l].astype(BF16),
                        w_out[l].astype(BF16))
        x = _ffn_call(x, mod[l], norm_g[l, 2], ffn2_up[l].astype(BF16), ffn2_down[l].astype(BF16), 6,
                      final_g=final_g if l == depth - 1 else None)
    return x
```

```python
import functools
import math

import jax
import jax.numpy as jnp
from jax import lax
from jax.experimental import pallas as pl
from jax.experimental.pallas import tpu as pltpu

F32 = jnp.float32
BF16 = jnp.bfloat16

LANES = 128
SUBLANES = 8
VMEM_LIMIT_BYTES = 56 * 1024 * 1024

D_MODEL = 1024
HEAD_DIM = 128
N_Q_HEADS = 8
N_KV_HEADS = 2
GROUP = N_Q_HEADS // N_KV_HEADS
ATTN_WIDTH = N_Q_HEADS * HEAD_DIM
KV_WIDTH = N_KV_HEADS * HEAD_DIM
ROPE_THETA = 10000.0
GRID_W = 64
LRU_WIDTH = D_MODEL
LRU_BLOCKS = 8
LRU_BLOCK_W = LRU_WIDTH // LRU_BLOCKS
LRU_C = 8.0
D_FF = 2816
FFN_RES = 0.5
N_MOD = 9
EPS = 1e-6

OFF_Q = 0
OFF_K = ATTN_WIDTH
OFF_V = OFF_K + KV_WIDTH
OFF_LX = OFF_V + KV_WIDTH
OFF_LG = OFF_LX + LRU_WIDTH
OFF_GATES = OFF_LG + LRU_WIDTH
IN_COLS = OFF_GATES + 2 * D_MODEL

ADA_TN = 2304
FFN_TM = 1024
FFN_TF = 512
FFN_CHUNKS = tuple((lo, min(lo + FFN_TF, D_FF)) for lo in range(0, D_FF, FFN_TF))
PROJ_TM = 512
ATT_TQ = 2048
ATT_M = 512
ATT_KC = 512
LRU_TC = 128
LRU_HALO = SUBLANES
LRU_PITCH = LRU_TC + SUBLANES
MERGE_TM = 512


def _cparams(semantics):
    return pltpu.CompilerParams(dimension_semantics=semantics, vmem_limit_bytes=VMEM_LIMIT_BYTES)


def _sigmoid(x):
    return 1.0 / (1.0 + jnp.exp(-x))


def _rms_modulate(x, g, shift, scale):
    y = x * lax.rsqrt(jnp.mean(x * x, axis=-1, keepdims=True) + EPS) * g
    return y * (1.0 + scale) + shift


def _ada_kernel(c_ref, w_ref, b_ref, o_ref):
    c = c_ref[...]
    c_act = (c * _sigmoid(c)).astype(BF16)
    o_ref[0] = jnp.dot(c_act, w_ref[0].astype(BF16), preferred_element_type=F32) + b_ref[0]


def _ada_call(c, ada_w, ada_b):
    depth, d, n = ada_w.shape
    b = c.shape[0]
    return pl.pallas_call(
        _ada_kernel,
        grid=(depth, n // ADA_TN),
        in_specs=[
            pl.BlockSpec((b, d), lambda l, j: (0, 0)),
            pl.BlockSpec((1, d, ADA_TN), lambda l, j: (l, 0, j)),
            pl.BlockSpec((1, 1, ADA_TN), lambda l, j: (l, 0, j)),
        ],
        out_specs=pl.BlockSpec((1, b, ADA_TN), lambda l, j: (l, 0, j)),
        out_shape=jax.ShapeDtypeStruct((depth, b, n), F32),
        compiler_params=_cparams(("arbitrary", "arbitrary")),
        name="ada_mod",
    )(c, ada_w, ada_b.reshape(depth, 1, n))


def _ffn_kernel(*refs, mod_base, final_norm):
    if final_norm:
        x_ref, mod_ref, g_ref, wup_ref, wd_ref, fg_ref, o_ref, acc_s = refs
    else:
        x_ref, mod_ref, g_ref, wup_ref, wd_ref, o_ref, acc_s = refs
    shift = mod_ref[0, mod_base:mod_base + 1, :]
    scale = mod_ref[0, mod_base + 1:mod_base + 2, :]
    h = _rms_modulate(x_ref[0], g_ref[...], shift, scale).astype(BF16)
    for n, (lo, hi) in enumerate(FFN_CHUNKS):
        gate = jnp.dot(h, wup_ref[:, lo:hi], preferred_element_type=F32)
        up = jnp.dot(h, wup_ref[:, D_FF + lo:D_FF + hi], preferred_element_type=F32)
        act = (gate * _sigmoid(gate) * up).astype(BF16)
        part = jnp.dot(act, wd_ref[lo:hi, :], preferred_element_type=F32)
        if n == 0:
            acc_s[...] = part
        else:
            acc_s[...] += part
    res_gate = mod_ref[0, mod_base + 2:mod_base + 3, :]
    out = x_ref[0] + (FFN_RES * res_gate) * acc_s[...]
    if final_norm:
        out = out * lax.rsqrt(jnp.mean(out * out, axis=-1, keepdims=True) + EPS) * fg_ref[...]
    o_ref[0] = out


def _ffn_call(x, mod, g, w_up, w_down, mod_base, final_g=None):
    b, s, d = x.shape
    final_norm = final_g is not None
    const2 = lambda bi, i: (0, 0)
    in_specs = [
        pl.BlockSpec((1, FFN_TM, d), lambda bi, i: (bi, i, 0)),
        pl.BlockSpec((1, N_MOD, d), lambda bi, i: (bi, 0, 0)),
        pl.BlockSpec((1, d), const2),
        pl.BlockSpec(w_up.shape, const2, pipeline_mode=pl.Buffered(1)),
        pl.BlockSpec(w_down.shape, const2, pipeline_mode=pl.Buffered(1)),
    ]
    args = [x, mod, g.reshape(1, d), w_up, w_down]
    if final_norm:
        in_specs.append(pl.BlockSpec((1, d), const2))
        args.append(final_g.reshape(1, d))
    return pl.pallas_call(
        functools.partial(_ffn_kernel, mod_base=mod_base, final_norm=final_norm),
        grid=(b, s // FFN_TM),
        in_specs=in_specs,
        out_specs=pl.BlockSpec((1, FFN_TM, d), lambda bi, i: (bi, i, 0)),
        out_shape=jax.ShapeDtypeStruct((b, s, d), F32),
        scratch_shapes=[pltpu.VMEM((FFN_TM, d), F32)],
        compiler_params=_cparams(("arbitrary", "arbitrary")),
        name="ffn_final" if final_norm else "ffn",
    )(*args)


def _head_norm_rope(xh, g, cos, sin):
    y = xh * lax.rsqrt(jnp.mean(xh * xh, axis=-1, keepdims=True) + EPS) * g
    return y * cos + pltpu.roll(y, HEAD_DIM // 2, 1) * sin


def _proj_kernel(x_ref, mod_ref, g_ref, w_ref, qg_ref, kg_ref, cos_ref, sin_ref,
                 q_ref, k_ref, v_ref, lx_ref, lg_ref, gates_ref):
    h = _rms_modulate(x_ref[0], g_ref[...], mod_ref[0, 3:4, :], mod_ref[0, 4:5, :]).astype(BF16)
    cos = cos_ref[...]
    sin = sin_ref[...]

    q = jnp.dot(h, w_ref[:, OFF_Q:OFF_K], preferred_element_type=F32)
    for hd in range(N_Q_HEADS):
        qh = q[:, hd * HEAD_DIM:(hd + 1) * HEAD_DIM]
        q_ref[0, hd] = _head_norm_rope(qh, qg_ref[...], cos, sin).astype(BF16)

    kv = jnp.dot(h, w_ref[:, OFF_K:OFF_LX], preferred_element_type=F32)
    for hd in range(N_KV_HEADS):
        kh = kv[:, hd * HEAD_DIM:(hd + 1) * HEAD_DIM]
        k_ref[0, hd] = _head_norm_rope(kh, kg_ref[...], cos, sin).astype(BF16)
        v_ref[0, hd] = kv[:, KV_WIDTH + hd * HEAD_DIM:KV_WIDTH + (hd + 1) * HEAD_DIM].astype(BF16)

    lx_ref[0] = jnp.dot(h, w_ref[:, OFF_LX:OFF_LG], preferred_element_type=F32)
    lg_ref[0] = jnp.dot(h, w_ref[:, OFF_LG:OFF_GATES], preferred_element_type=F32)
    gates_ref[0] = jnp.dot(h, w_ref[:, OFF_GATES:IN_COLS], preferred_element_type=F32)


def _proj_call(x, mod, g, w_in, q_g, k_g, cos, sin):
    b, s, d = x.shape
    tm = PROJ_TM
    const2 = lambda bi, i: (0, 0)
    return pl.pallas_call(
        _proj_kernel,
        grid=(b, s // tm),
        in_specs=[
            pl.BlockSpec((1, tm, d), lambda bi, i: (bi, i, 0)),
            pl.BlockSpec((1, N_MOD, d), lambda bi, i: (bi, 0, 0)),
            pl.BlockSpec((1, d), const2),
            pl.BlockSpec((d, IN_COLS), const2, pipeline_mode=pl.Buffered(1)),
            pl.BlockSpec((1, HEAD_DIM), const2),
            pl.BlockSpec((1, HEAD_DIM), const2),
            pl.BlockSpec((tm, HEAD_DIM), lambda bi, i: (i, 0)),
            pl.BlockSpec((tm, HEAD_DIM), lambda bi, i: (i, 0)),
        ],
        out_specs=[
            pl.BlockSpec((1, N_Q_HEADS, tm, HEAD_DIM), lambda bi, i: (bi, 0, i, 0)),
            pl.BlockSpec((1, N_KV_HEADS, tm, HEAD_DIM), lambda bi, i: (bi, 0, i, 0)),
            pl.BlockSpec((1, N_KV_HEADS, tm, HEAD_DIM), lambda bi, i: (bi, 0, i, 0)),
            pl.BlockSpec((1, tm, LRU_WIDTH), lambda bi, i: (bi, i, 0)),
            pl.BlockSpec((1, tm, LRU_WIDTH), lambda bi, i: (bi, i, 0)),
            pl.BlockSpec((1, tm, 2 * d), lambda bi, i: (bi, i, 0)),
        ],
        out_shape=[
            jax.ShapeDtypeStruct((b, N_Q_HEADS, s, HEAD_DIM), BF16),
            jax.ShapeDtypeStruct((b, N_KV_HEADS, s, HEAD_DIM), BF16),
            jax.ShapeDtypeStruct((b, N_KV_HEADS, s, HEAD_DIM), BF16),
            jax.ShapeDtypeStruct((b, s, LRU_WIDTH), F32),
            jax.ShapeDtypeStruct((b, s, LRU_WIDTH), F32),
            jax.ShapeDtypeStruct((b, s, 2 * d), F32),
        ],
        compiler_params=_cparams(("arbitrary", "arbitrary")),
        name="in_proj",
    )(x, mod, g.reshape(1, d), w_in, q_g.reshape(1, HEAD_DIM), k_g.reshape(1, HEAD_DIM), cos, sin)


def _attn_kernel(q_ref, k_ref, v_ref, o_ref, s_scr, m_scr):
    n_kc = k_ref.shape[2] // ATT_KC
    row_blocks = ATT_TQ // ATT_M
    n_items = GROUP * row_blocks
    exp2_scale = (HEAD_DIM ** -0.5) * math.log2(math.e)

    def item_rows(item):
        g = item // row_blocks
        r0 = pl.multiple_of((item % row_blocks) * ATT_M, ATT_M)
        return g, pl.ds(r0, ATT_M)

    def scores(item, slot):
        g, rows = item_rows(item)
        q = q_ref[0, g, rows, :]
        m = None
        for c in range(n_kc):
            kc = k_ref[0, 0, c * ATT_KC:(c + 1) * ATT_KC, :]
            s = lax.dot_general(q, kc, (((1,), (1,)), ((), ())), preferred_element_type=F32)
            s_scr[slot, :, c * ATT_KC:(c + 1) * ATT_KC] = s
            mc = jnp.max(s, axis=-1, keepdims=True)
            m = mc if m is None else jnp.maximum(m, mc)
        m_scr[slot] = m

    def finish(item, slot):
        g, rows = item_rows(item)
        m = m_scr[slot]
        l = jnp.zeros((ATT_M, 1), F32)
        acc = jnp.zeros((ATT_M, HEAD_DIM), F32)
        for c in range(n_kc):
            p = jnp.exp2((s_scr[slot, :, c * ATT_KC:(c + 1) * ATT_KC] - m) * exp2_scale)
            l = l + jnp.sum(p, axis=-1, keepdims=True)
            vc = v_ref[0, 0, c * ATT_KC:(c + 1) * ATT_KC, :]
            acc = acc + jnp.dot(p.astype(BF16), vc, preferred_element_type=F32)
        o_ref[0, g, rows, :] = (acc / l).astype(BF16)

    scores(jnp.int32(0), 0)

    def pair_body(j, carry):
        first = 2 * j
        scores(first + 1, 1)
        finish(first, 0)
        scores(jnp.minimum(first + 2, n_items - 1), 0)
        finish(first + 1, 1)
        return carry

    lax.fori_loop(0, n_items // 2, pair_body, 0)


def _attn_call(q, k, v):
    b, _, s, _ = q.shape
    return pl.pallas_call(
        _attn_kernel,
        grid=(b, N_KV_HEADS, s // ATT_TQ),
        in_specs=[
            pl.BlockSpec((1, GROUP, ATT_TQ, HEAD_DIM), lambda bi, kh, i: (bi, kh, i, 0)),
            pl.BlockSpec((1, 1, s, HEAD_DIM), lambda bi, kh, i: (bi, kh, 0, 0)),
            pl.BlockSpec((1, 1, s, HEAD_DIM), lambda bi, kh, i: (bi, kh, 0, 0)),
        ],
        out_specs=pl.BlockSpec((1, GROUP, ATT_TQ, HEAD_DIM), lambda bi, kh, i: (bi, kh, i, 0)),
        out_shape=jax.ShapeDtypeStruct((b, N_Q_HEADS, s, HEAD_DIM), BF16),
        scratch_shapes=[pltpu.VMEM((2, ATT_M, s), F32), pltpu.VMEM((2, ATT_M, 1), F32)],
        compiler_params=_cparams(("arbitrary", "arbitrary", "arbitrary")),
        name="attention",
    )(q, k, v)


def _gelu_tanh(x):
    return 0.5 * x * (1.0 + jnp.tanh(math.sqrt(2.0 / math.pi) * (x + 0.044715 * (x * x * x))))


def _lru_kernel(*refs, reverse):
    if reverse:
        (x_ref, prev_ref, next_ref, cw_ref, cb_ref, wg_ref, ba_ref, bx_ref, lam_ref, hf_ref, lg_ref,
         o_ref, xe_s, xc_s, a_s, u_s, hc_s) = refs
    else:
        (x_ref, prev_ref, next_ref, cw_ref, cb_ref, wg_ref, ba_ref, bx_ref, lam_ref,
         o_ref, xe_s, xc_s, a_s, u_s, hc_s) = refs
    nb = x_ref.shape[0]
    tc, pitch, halo = LRU_TC, LRU_PITCH, LRU_HALO
    i = pl.program_id(0)
    n = pl.num_programs(0)
    chunk = (n - 1 - i) if reverse else i

    @pl.when(i == 0)
    def _():
        hc_s[...] = jnp.zeros_like(hc_s)

    xe_s[:, 0:halo, :] = jnp.where(chunk == 0, 0.0, prev_ref[...])
    xe_s[:, halo:halo + tc, :] = x_ref[...]
    xe_s[:, halo + tc:2 * halo + tc, :] = jnp.where(chunk == n - 1, 0.0, next_ref[...])
    for b in range(nb):
        xc = cb_ref[...] + xe_s[b, halo - 2:halo - 2 + tc, :] * cw_ref[0:1, :]
        xc = xc + xe_s[b, halo - 1:halo - 1 + tc, :] * cw_ref[1:2, :]
        xc = xc + xe_s[b, halo:halo + tc, :] * cw_ref[2:3, :]
        xc = xc + xe_s[b, halo + 1:halo + 1 + tc, :] * cw_ref[3:4, :]
        xc_s[b * tc:(b + 1) * tc, :] = xc

    for k in range(LRU_BLOCKS):
        cols = slice(k * LRU_BLOCK_W, (k + 1) * LRU_BLOCK_W)
        xb = xc_s[:, cols]
        pre = jnp.dot(xb.astype(BF16), wg_ref[k], preferred_element_type=F32)
        r = _sigmoid(pre[:, :LRU_BLOCK_W] + ba_ref[:, cols])
        ig = _sigmoid(pre[:, LRU_BLOCK_W:] + bx_ref[:, cols])
        z = -lam_ref[:, cols]
        softplus = jnp.maximum(z, 0.0) + jnp.log1p(jnp.exp(-jnp.abs(z)))
        a = jnp.exp(-LRU_C * r * softplus)
        u = jnp.sqrt(1.0 - a * a) * (ig * xb)
        for b in range(nb):
            a_s[k, b * pitch:b * pitch + tc, :] = a[b * tc:(b + 1) * tc]
            u_s[k, b * pitch:b * pitch + tc, :] = u[b * tc:(b + 1) * tc]

    def step(t, hs):
        tt = (tc - 1 - t) if reverse else t
        new = []
        for k in range(LRU_BLOCKS):
            rows = pl.ds(tt, nb, stride=pitch)
            hk = a_s[k, rows, :] * hs[k] + u_s[k, rows, :]
            u_s[k, rows, :] = hk
            new.append(hk)
        return tuple(new)

    hs = lax.fori_loop(0, tc, step, tuple(hc_s[k] for k in range(LRU_BLOCKS)), unroll=4)
    for k in range(LRU_BLOCKS):
        hc_s[k] = hs[k]

    for k in range(LRU_BLOCKS):
        cols = slice(k * LRU_BLOCK_W, (k + 1) * LRU_BLOCK_W)
        for b in range(nb):
            hk = u_s[k, b * pitch:b * pitch + tc, :]
            if reverse:
                o_ref[b, :, cols] = ((hf_ref[b, :, cols] + hk) * _gelu_tanh(lg_ref[b, :, cols])).astype(BF16)
            else:
                o_ref[b, :, cols] = hk


def _lru_call(lx, conv_w, conv_b, w_gate, b_a, b_x, lam, reverse, hf=None, lg=None):
    b, s, w = lx.shape
    tc, halo = LRU_TC, LRU_HALO
    n = s // tc
    hb_per_chunk = tc // halo
    last_hb = s // halo - 1
    cidx = (lambda i: n - 1 - i) if reverse else (lambda i: i)
    const2 = lambda i: (0, 0)
    chunk_spec = pl.BlockSpec((b, tc, w), lambda i: (0, cidx(i), 0))
    in_specs = [
        chunk_spec,
        pl.BlockSpec((b, halo, w), lambda i: (0, jnp.maximum(cidx(i) * hb_per_chunk - 1, 0), 0)),
        pl.BlockSpec((b, halo, w), lambda i: (0, jnp.minimum((cidx(i) + 1) * hb_per_chunk, last_hb), 0)),
        pl.BlockSpec(conv_w.shape, const2),
        pl.BlockSpec((1, w), const2),
        pl.BlockSpec(w_gate.shape, lambda i: (0, 0, 0)),
        pl.BlockSpec((1, w), const2),
        pl.BlockSpec((1, w), const2),
        pl.BlockSpec((1, w), const2),
    ]
    args = [lx, lx, lx, conv_w, conv_b.reshape(1, w), w_gate, b_a.reshape(1, w), b_x.reshape(1, w),
            lam.reshape(1, w)]
    if reverse:
        in_specs += [chunk_spec, chunk_spec]
        args += [hf, lg]
    return pl.pallas_call(
        functools.partial(_lru_kernel, reverse=reverse),
        grid=(n,),
        in_specs=in_specs,
        out_specs=chunk_spec,
        out_shape=jax.ShapeDtypeStruct((b, s, w), BF16 if reverse else F32),
        scratch_shapes=[
            pltpu.VMEM((b, tc + 2 * halo, w), F32),
            pltpu.VMEM((b * tc, w), F32),
            pltpu.VMEM((LRU_BLOCKS, b * LRU_PITCH, LRU_BLOCK_W), F32),
            pltpu.VMEM((LRU_BLOCKS, b * LRU_PITCH, LRU_BLOCK_W), F32),
            pltpu.VMEM((LRU_BLOCKS, b, LRU_BLOCK_W), F32),
        ],
        compiler_params=_cparams(("arbitrary",)),
        name="lru_bwd" if reverse else "lru_fwd",
    )(*args)


def _merge_kernel(attn_ref, lru_ref, gates_ref, x_ref, mod_ref, wa_ref, wl_ref, wo_ref, o_ref, acat_s):
    for hd in range(N_Q_HEADS):
        acat_s[:, hd * HEAD_DIM:(hd + 1) * HEAD_DIM] = attn_ref[0, hd]
    a = jnp.dot(acat_s[...], wa_ref[...], preferred_element_type=F32)
    r = jnp.dot(lru_ref[0], wl_ref[...], preferred_element_type=F32)
    g = _sigmoid(gates_ref[0])
    merged = g[:, :D_MODEL] * a + g[:, D_MODEL:] * r
    y = jnp.dot(merged.astype(BF16), wo_ref[...], preferred_element_type=F32)
    o_ref[0] = x_ref[0] + mod_ref[0, 5:6, :] * y


def _merge_call(attn, lru, gates, x, mod, w_attn_o, w_lru_o, w_out):
    b, s, d = x.shape
    tm = MERGE_TM
    const2 = lambda bi, i: (0, 0)
    tok = lambda bi, i: (bi, i, 0)
    return pl.pallas_call(
        _merge_kernel,
        grid=(b, s // tm),
        in_specs=[
            pl.BlockSpec((1, N_Q_HEADS, tm, HEAD_DIM), lambda bi, i: (bi, 0, i, 0)),
            pl.BlockSpec((1, tm, LRU_WIDTH), tok),
            pl.BlockSpec((1, tm, 2 * d), tok),
            pl.BlockSpec((1, tm, d), tok),
            pl.BlockSpec((1, N_MOD, d), lambda bi, i: (bi, 0, 0)),
            pl.BlockSpec((ATTN_WIDTH, d), const2),
            pl.BlockSpec((LRU_WIDTH, d), const2),
            pl.BlockSpec((d, d), const2),
        ],
        out_specs=pl.BlockSpec((1, tm, d), tok),
        out_shape=jax.ShapeDtypeStruct((b, s, d), F32),
        scratch_shapes=[pltpu.VMEM((tm, ATTN_WIDTH), BF16)],
        compiler_params=_cparams(("arbitrary", "arbitrary")),
        name="merge_out",
    )(attn, lru, gates, x, mod, w_attn_o, w_lru_o, w_out)


def _rope_tables(s):
    rows = s // GRID_W
    row_ids = jnp.broadcast_to(jnp.arange(rows, dtype=F32)[:, None], (rows, GRID_W)).reshape(s)
    col_ids = jnp.broadcast_to(jnp.arange(GRID_W, dtype=F32)[None, :], (rows, GRID_W)).reshape(s)
    axis_dim = HEAD_DIM // 2
    inv_freq = ROPE_THETA ** (-jnp.arange(0, axis_dim, 2, dtype=F32) / axis_dim)
    ang = jnp.concatenate([row_ids[:, None] * inv_freq, col_ids[:, None] * inv_freq], axis=-1)
    cos, sin = jnp.cos(ang), jnp.sin(ang)
    return jnp.concatenate([cos, cos], axis=-1), jnp.concatenate([-sin, sin], axis=-1)


def _deinterleave_heads(w):
    lead = w.shape[:-1]
    n_heads = w.shape[-1] // HEAD_DIM
    w = w.reshape(*lead, n_heads, HEAD_DIM // 2, 2)
    return jnp.swapaxes(w, -1, -2).reshape(*lead, n_heads * HEAD_DIM)


def kernel(x, c, ada_w, ada_b, norm_g, ffn1_up, ffn1_down, w_in, q_norm_g, k_norm_g, conv_w, conv_b,
           lru_wa, lru_ba, lru_wx, lru_bx, lru_lambda, w_attn_o, w_lru_o, w_out, ffn2_up, ffn2_down, final_g):
    b, s, d = x.shape
    depth = ada_w.shape[0]
    mod = _ada_call(c, ada_w, ada_b).reshape(depth, b, N_MOD, d)
    cos, sin = _rope_tables(s)

    for l in range(depth):
        w_in_l = jnp.concatenate([_deinterleave_heads(w_in[l][:, :OFF_V]), w_in[l][:, OFF_V:]],
                                 axis=-1).astype(BF16)
        x = _ffn_call(x, mod[l], norm_g[l, 0], ffn1_up[l].astype(BF16), ffn1_down[l].astype(BF16), 0)

        q, k, v, lx, lg, gates = _proj_call(x, mod[l], norm_g[l, 1], w_in_l, _deinterleave_heads(q_norm_g[l]),
                                            _deinterleave_heads(k_norm_g[l]), cos, sin)
        attn = _attn_call(q, k, v)
        w_gate = jnp.concatenate([lru_wa[l], lru_wx[l]], axis=-1).astype(BF16)
        hf = _lru_call(lx, conv_w[l], conv_b[l], w_gate[0], lru_ba[l, 0], lru_bx[l, 0], lru_lambda[l, 0], False)
        lru = _lru_call(lx, conv_w[l], conv_b[l], w_gate[1], lru_ba[l, 1], lru_bx[l, 1], lru_lambda[l, 1], True,
                        hf=hf, lg=lg)
        x = _merge_call(attn, lru, gates, x, mod[l], w_attn_o[l].astype(BF16), w_lru_o[l].astype(BF16),
                        w_out[l].astype(BF16))
        x = _ffn_call(x, mod[l], norm_g[l, 2], ffn2_up[l].astype(BF16), ffn2_down[l].astype(BF16), 6,
                      final_g=final_g if l == depth - 1 else None)
    return x
```

```python
import functools
import math

import jax
import jax.numpy as jnp
from jax import lax
from jax.experimental import pallas as pl
from jax.experimental.pallas import tpu as pltpu

F32 = jnp.float32
BF16 = jnp.bfloat16

LANES = 128
SUBLANES = 8
VMEM_LIMIT_BYTES = 56 * 1024 * 1024

D_MODEL = 1024
HEAD_DIM = 128
N_Q_HEADS = 8
N_KV_HEADS = 2
GROUP = N_Q_HEADS // N_KV_HEADS
ATTN_WIDTH = N_Q_HEADS * HEAD_DIM
KV_WIDTH = N_KV_HEADS * HEAD_DIM
ROPE_THETA = 10000.0
GRID_W = 64
LRU_WIDTH = D_MODEL
LRU_BLOCKS = 8
LRU_BLOCK_W = LRU_WIDTH // LRU_BLOCKS
LRU_C = 8.0
D_FF = 2816
FFN_RES = 0.5
N_MOD = 9
EPS = 1e-6
Q_PRESCALE = (HEAD_DIM ** -0.5) * math.log2(math.e)

OFF_Q = 0
OFF_K = ATTN_WIDTH
OFF_V = OFF_K + KV_WIDTH
OFF_LX = OFF_V + KV_WIDTH
OFF_LG = OFF_LX + LRU_WIDTH
OFF_GATES = OFF_LG + LRU_WIDTH
IN_COLS = OFF_GATES + 2 * D_MODEL

ADA_TN = 2304
FFN_TM = 1024
FFN_TF = 512
FFN_CHUNKS = tuple((lo, min(lo + FFN_TF, D_FF)) for lo in range(0, D_FF, FFN_TF))
PROJ_TM = 512
ATT_TQ = 4096
ATT_M = 512
ATT_KC = 512
LRU_TC = 128
LRU_HALO = SUBLANES
LRU_PITCH = LRU_TC + 2 * LRU_HALO + SUBLANES
MERGE_TM = 512


def _cparams(semantics):
    return pltpu.CompilerParams(dimension_semantics=semantics, vmem_limit_bytes=VMEM_LIMIT_BYTES)


def _sigmoid(x):
    return 1.0 / (1.0 + jnp.exp2(x * (-math.log2(math.e))))


def _rms_modulate(x, g, shift, scale):
    y = x * lax.rsqrt(jnp.mean(x * x, axis=-1, keepdims=True) + EPS) * g
    return y * (1.0 + scale) + shift


def _ada_kernel(c_ref, w_ref, b_ref, o_ref):
    c = c_ref[...]
    c_act = (c * _sigmoid(c)).astype(BF16)
    o_ref[0] = jnp.dot(c_act, w_ref[0].astype(BF16), preferred_element_type=F32) + b_ref[0]


def _ada_call(c, ada_w, ada_b):
    depth, d, n = ada_w.shape
    b = c.shape[0]
    return pl.pallas_call(
        _ada_kernel,
        grid=(depth, n // ADA_TN),
        in_specs=[
            pl.BlockSpec((b, d), lambda l, j: (0, 0)),
            pl.BlockSpec((1, d, ADA_TN), lambda l, j: (l, 0, j)),
            pl.BlockSpec((1, 1, ADA_TN), lambda l, j: (l, 0, j)),
        ],
        out_specs=pl.BlockSpec((1, b, ADA_TN), lambda l, j: (l, 0, j)),
        out_shape=jax.ShapeDtypeStruct((depth, b, n), F32),
        compiler_params=_cparams(("arbitrary", "arbitrary")),
        name="ada_mod",
    )(c, ada_w, ada_b.reshape(depth, 1, n))


def _ffn_kernel(*refs, mod_base, final_norm):
    if final_norm:
        x_ref, mod_ref, g_ref, wup_ref, wd_ref, fg_ref, o_ref, acc_s = refs
    else:
        x_ref, mod_ref, g_ref, wup_ref, wd_ref, o_ref, acc_s = refs
    shift = mod_ref[0, mod_base:mod_base + 1, :]
    scale = mod_ref[0, mod_base + 1:mod_base + 2, :]
    h = _rms_modulate(x_ref[0], g_ref[...], shift, scale).astype(BF16)
    for n, (lo, hi) in enumerate(FFN_CHUNKS):
        gate = jnp.dot(h, wup_ref[:, lo:hi], preferred_element_type=F32)
        up = jnp.dot(h, wup_ref[:, D_FF + lo:D_FF + hi], preferred_element_type=F32)
        act = (gate * _sigmoid(gate) * up).astype(BF16)
        part = jnp.dot(act, wd_ref[lo:hi, :], preferred_element_type=F32)
        if n == 0:
            acc_s[...] = part
        else:
            acc_s[...] += part
    res_gate = mod_ref[0, mod_base + 2:mod_base + 3, :]
    out = x_ref[0] + (FFN_RES * res_gate) * acc_s[...]
    if final_norm:
        out = out * lax.rsqrt(jnp.mean(out * out, axis=-1, keepdims=True) + EPS) * fg_ref[...]
    o_ref[0] = out


def _ffn_call(x, mod, g, w_up, w_down, mod_base, final_g=None):
    b, s, d = x.shape
    final_norm = final_g is not None
    const2 = lambda bi, i: (0, 0)
    in_specs = [
        pl.BlockSpec((1, FFN_TM, d), lambda bi, i: (bi, i, 0)),
        pl.BlockSpec((1, N_MOD, d), lambda bi, i: (bi, 0, 0)),
        pl.BlockSpec((1, d), const2),
        pl.BlockSpec(w_up.shape, const2, pipeline_mode=pl.Buffered(1)),
        pl.BlockSpec(w_down.shape, const2, pipeline_mode=pl.Buffered(1)),
    ]
    args = [x, mod, g.reshape(1, d), w_up, w_down]
    if final_norm:
        in_specs.append(pl.BlockSpec((1, d), const2))
        args.append(final_g.reshape(1, d))
    return pl.pallas_call(
        functools.partial(_ffn_kernel, mod_base=mod_base, final_norm=final_norm),
        grid=(b, s // FFN_TM),
        in_specs=in_specs,
        out_specs=pl.BlockSpec((1, FFN_TM, d), lambda bi, i: (bi, i, 0)),
        out_shape=jax.ShapeDtypeStruct((b, s, d), F32),
        scratch_shapes=[pltpu.VMEM((FFN_TM, d), F32)],
        compiler_params=_cparams(("arbitrary", "arbitrary")),
        name="ffn_final" if final_norm else "ffn",
    )(*args)


def _head_norm_rope(xh, g, cos, sin):
    y = xh * lax.rsqrt(jnp.mean(xh * xh, axis=-1, keepdims=True) + EPS) * g
    return y * cos + pltpu.roll(y, HEAD_DIM // 2, 1) * sin


def _proj_kernel(x_ref, mod_ref, g_ref, w_ref, qg_ref, kg_ref, cos_ref, sin_ref,
                 q_ref, k_ref, v_ref, lx_ref, lg_ref, gates_ref):
    h = _rms_modulate(x_ref[0], g_ref[...], mod_ref[0, 3:4, :], mod_ref[0, 4:5, :]).astype(BF16)
    cos = cos_ref[...]
    sin = sin_ref[...]

    q = jnp.dot(h, w_ref[:, OFF_Q:OFF_K], preferred_element_type=F32)
    for hd in range(N_Q_HEADS):
        qh = q[:, hd * HEAD_DIM:(hd + 1) * HEAD_DIM]
        q_ref[0, hd] = (_head_norm_rope(qh, qg_ref[...], cos, sin) * Q_PRESCALE).astype(BF16)

    kv = jnp.dot(h, w_ref[:, OFF_K:OFF_LX], preferred_element_type=F32)
    for hd in range(N_KV_HEADS):
        kh = kv[:, hd * HEAD_DIM:(hd + 1) * HEAD_DIM]
        k_ref[0, hd] = _head_norm_rope(kh, kg_ref[...], cos, sin).astype(BF16)
        v_ref[0, hd] = kv[:, KV_WIDTH + hd * HEAD_DIM:KV_WIDTH + (hd + 1) * HEAD_DIM].astype(BF16)

    lx_ref[0] = jnp.dot(h, w_ref[:, OFF_LX:OFF_LG], preferred_element_type=F32)
    lg_ref[0] = jnp.dot(h, w_ref[:, OFF_LG:OFF_GATES], preferred_element_type=F32)
    gates_ref[0] = jnp.dot(h, w_ref[:, OFF_GATES:IN_COLS], preferred_element_type=F32)


def _proj_call(x, mod, g, w_in, q_g, k_g, cos, sin):
    b, s, d = x.shape
    tm = PROJ_TM
    const2 = lambda bi, i: (0, 0)
    return pl.pallas_call(
        _proj_kernel,
        grid=(b, s // tm),
        in_specs=[
            pl.BlockSpec((1, tm, d), lambda bi, i: (bi, i, 0)),
            pl.BlockSpec((1, N_MOD, d), lambda bi, i: (bi, 0, 0)),
            pl.BlockSpec((1, d), const2),
            pl.BlockSpec((d, IN_COLS), const2, pipeline_mode=pl.Buffered(1)),
            pl.BlockSpec((1, HEAD_DIM), const2),
            pl.BlockSpec((1, HEAD_DIM), const2),
            pl.BlockSpec((tm, HEAD_DIM), lambda bi, i: (i, 0)),
            pl.BlockSpec((tm, HEAD_DIM), lambda bi, i: (i, 0)),
        ],
        out_specs=[
            pl.BlockSpec((1, N_Q_HEADS, tm, HEAD_DIM), lambda bi, i: (bi, 0, i, 0)),
            pl.BlockSpec((1, N_KV_HEADS, tm, HEAD_DIM), lambda bi, i: (bi, 0, i, 0)),
            pl.BlockSpec((1, N_KV_HEADS, tm, HEAD_DIM), lambda bi, i: (bi, 0, i, 0)),
            pl.BlockSpec((1, tm, LRU_WIDTH), lambda bi, i: (bi, i, 0)),
            pl.BlockSpec((1, tm, LRU_WIDTH), lambda bi, i: (bi, i, 0)),
            pl.BlockSpec((1, tm, 2 * d), lambda bi, i: (bi, i, 0)),
        ],
        out_shape=[
            jax.ShapeDtypeStruct((b, N_Q_HEADS, s, HEAD_DIM), BF16),
            jax.ShapeDtypeStruct((b, N_KV_HEADS, s, HEAD_DIM), BF16),
            jax.ShapeDtypeStruct((b, N_KV_HEADS, s, HEAD_DIM), BF16),
            jax.ShapeDtypeStruct((b, s, LRU_WIDTH), F32),
            jax.ShapeDtypeStruct((b, s, LRU_WIDTH), F32),
            jax.ShapeDtypeStruct((b, s, 2 * d), F32),
        ],
        compiler_params=_cparams(("arbitrary", "arbitrary")),
        name="in_proj",
    )(x, mod, g.reshape(1, d), w_in, q_g.reshape(1, HEAD_DIM), k_g.reshape(1, HEAD_DIM), cos, sin)


def _attn_kernel(q_ref, k_ref, v_ref, o_ref, s_scr, m_scr, vext_s):
    n_kc = k_ref.shape[2] // ATT_KC
    row_blocks = ATT_TQ // ATT_M
    n_items = GROUP * row_blocks

    vext_s[:, :HEAD_DIM] = v_ref[0, 0]
    vext_s[:, HEAD_DIM:] = jnp.ones((vext_s.shape[0], HEAD_DIM), BF16)

    def item_rows(item):
        g = item // row_blocks
        r0 = pl.multiple_of((item % row_blocks) * ATT_M, ATT_M)
        return g, pl.ds(r0, ATT_M)

    def scores(item, slot):
        g, rows = item_rows(item)
        q = q_ref[0, g, rows, :]
        m = None
        for c in range(n_kc):
            kc = k_ref[0, 0, c * ATT_KC:(c + 1) * ATT_KC, :]
            s = lax.dot_general(q, kc, (((1,), (1,)), ((), ())), preferred_element_type=F32)
            s_scr[slot, :, c * ATT_KC:(c + 1) * ATT_KC] = s
            mc = jnp.max(s, axis=-1, keepdims=True)
            m = mc if m is None else jnp.maximum(m, mc)
        m_scr[slot] = m

    def finish(item, slot):
        g, rows = item_rows(item)
        m = m_scr[slot]
        acc = jnp.zeros((ATT_M, 2 * HEAD_DIM), F32)
        for c in range(n_kc):
            p = jnp.exp2(s_scr[slot, :, c * ATT_KC:(c + 1) * ATT_KC] - m)
            vc = vext_s[c * ATT_KC:(c + 1) * ATT_KC, :]
            acc = acc + jnp.dot(p.astype(BF16), vc, preferred_element_type=F32)
        o_ref[0, g, rows, :] = (acc[:, :HEAD_DIM] / acc[:, HEAD_DIM:]).astype(BF16)

    scores(jnp.int32(0), 0)

    def pair_body(j, carry):
        first = 2 * j
        scores(first + 1, 1)
        finish(first, 0)
        scores(jnp.minimum(first + 2, n_items - 1), 0)
        finish(first + 1, 1)
        return carry

    lax.fori_loop(0, n_items // 2, pair_body, 0)


def _attn_call(q, k, v):
    b, _, s, _ = q.shape
    return pl.pallas_call(
        _attn_kernel,
        grid=(b, N_KV_HEADS, s // ATT_TQ),
        in_specs=[
            pl.BlockSpec((1, GROUP, ATT_TQ, HEAD_DIM), lambda bi, kh, i: (bi, kh, i, 0)),
            pl.BlockSpec((1, 1, s, HEAD_DIM), lambda bi, kh, i: (bi, kh, 0, 0)),
            pl.BlockSpec((1, 1, s, HEAD_DIM), lambda bi, kh, i: (bi, kh, 0, 0)),
        ],
        out_specs=pl.BlockSpec((1, GROUP, ATT_TQ, HEAD_DIM), lambda bi, kh, i: (bi, kh, i, 0)),
        out_shape=jax.ShapeDtypeStruct((b, N_Q_HEADS, s, HEAD_DIM), BF16),
        scratch_shapes=[pltpu.VMEM((2, ATT_M, s), F32), pltpu.VMEM((2, ATT_M, 1), F32),
                        pltpu.VMEM((s, 2 * HEAD_DIM), BF16)],
        compiler_params=_cparams(("arbitrary", "arbitrary", "arbitrary")),
        name="attention",
    )(q, k, v)


def _gelu_tanh(x):
    return 0.5 * x * (1.0 + jnp.tanh(math.sqrt(2.0 / math.pi) * (x + 0.044715 * (x * x * x))))


def _lru_kernel(*refs, reverse):
    if reverse:
        (x_ref, prev_ref, next_ref, cw_ref, cb_ref, wg_ref, ba_ref, bx_ref, lam_ref, hf_ref, lg_ref,
         o_ref, bm_s, xt_s, u_s, hc_s) = refs
    else:
        (x_ref, prev_ref, next_ref, cw_ref, cb_ref, wg_ref, ba_ref, bx_ref, lam_ref,
         o_ref, bm_s, xt_s, u_s, hc_s) = refs
    nb = x_ref.shape[0]
    tc, pitch, halo = LRU_TC, LRU_PITCH, LRU_HALO
    ext = tc + 2 * halo
    i = pl.program_id(0)
    n = pl.num_programs(0)
    chunk = (n - 1 - i) if reverse else i

    @pl.when(i == 0)
    def _():
        hc_s[...] = jnp.zeros_like(hc_s)

    for k in range(LRU_BLOCKS):
        cols = slice(k * LRU_BLOCK_W, (k + 1) * LRU_BLOCK_W)
        for b in range(nb):
            base = b * pitch
            bm_s[k, base:base + halo, :] = jnp.where(chunk == 0, 0.0, prev_ref[b, :, cols])
            bm_s[k, base + halo:base + halo + tc, :] = x_ref[b, :, cols]
            bm_s[k, base + halo + tc:base + ext, :] = jnp.where(chunk == n - 1, 0.0, next_ref[b, :, cols])

    def to_time_major(tau, carry):
        dst = pl.ds(pl.multiple_of(tau * nb, nb), nb)
        for k in range(LRU_BLOCKS):
            xt_s[k, dst, :] = bm_s[k, pl.ds(tau, nb, stride=pitch), :]
        return carry

    lax.fori_loop(0, ext, to_time_major, 0, unroll=8)

    for k in range(LRU_BLOCKS):
        cols = slice(k * LRU_BLOCK_W, (k + 1) * LRU_BLOCK_W)
        xb = cb_ref[:, cols]
        for j in range(cw_ref.shape[0]):
            r0 = (halo - 2 + j) * nb
            tap = xt_s[k, r0:r0 + tc * nb, :] * cw_ref[j:j + 1, cols]
            xb = xb + tap
        pre = jnp.dot(xb.astype(BF16), wg_ref[k], preferred_element_type=F32)
        r = _sigmoid(pre[:, :LRU_BLOCK_W] + ba_ref[:, cols])
        ig = _sigmoid(pre[:, LRU_BLOCK_W:] + bx_ref[:, cols])
        z = -lam_ref[:, cols]
        softplus = jnp.maximum(z, 0.0) + jnp.log1p(jnp.exp(-jnp.abs(z)))
        a = jnp.exp2(r * ((-LRU_C * math.log2(math.e)) * softplus))
        one_m_a2 = 1.0 - a * a
        root = jnp.where(one_m_a2 > 0.0, one_m_a2 * lax.rsqrt(one_m_a2), 0.0)
        bm_s[k, 0:tc * nb, :] = a
        u_s[k] = root * (ig * xb)

    def step(t, hs):
        tt = (tc - 1 - t) if reverse else t
        rows = pl.ds(pl.multiple_of(tt * nb, nb), nb)
        new = []
        for k in range(LRU_BLOCKS):
            hk = bm_s[k, rows, :] * hs[k] + u_s[k, rows, :]
            u_s[k, rows, :] = hk
            new.append(hk)
        return tuple(new)

    hs = lax.fori_loop(0, tc, step, tuple(hc_s[k] for k in range(LRU_BLOCKS)), unroll=8)
    for k in range(LRU_BLOCKS):
        hc_s[k] = hs[k]

    for k in range(LRU_BLOCKS):
        cols = slice(k * LRU_BLOCK_W, (k + 1) * LRU_BLOCK_W)
        for b in range(nb):
            hk = u_s[k, pl.ds(b, tc, stride=nb), :]
            if reverse:
                o_ref[b, :, cols] = ((hf_ref[b, :, cols] + hk) * _gelu_tanh(lg_ref[b, :, cols])).astype(BF16)
            else:
                o_ref[b, :, cols] = hk


def _lru_call(lx, conv_w, conv_b, w_gate, b_a, b_x, lam, reverse, hf=None, lg=None):
    b, s, w = lx.shape
    tc, halo = LRU_TC, LRU_HALO
    n = s // tc
    hb_per_chunk = tc // halo
    last_hb = s // halo - 1
    cidx = (lambda i: n - 1 - i) if reverse else (lambda i: i)
    const2 = lambda i: (0, 0)
    chunk_spec = pl.BlockSpec((b, tc, w), lambda i: (0, cidx(i), 0))
    in_specs = [
        chunk_spec,
        pl.BlockSpec((b, halo, w), lambda i: (0, jnp.maximum(cidx(i) * hb_per_chunk - 1, 0), 0)),
        pl.BlockSpec((b, halo, w), lambda i: (0, jnp.minimum((cidx(i) + 1) * hb_per_chunk, last_hb), 0)),
        pl.BlockSpec(conv_w.shape, const2),
        pl.BlockSpec((1, w), const2),
        pl.BlockSpec(w_gate.shape, lambda i: (0, 0, 0)),
        pl.BlockSpec((1, w), const2),
        pl.BlockSpec((1, w), const2),
        pl.BlockSpec((1, w), const2),
    ]
    args = [lx, lx, lx, conv_w, conv_b.reshape(1, w), w_gate, b_a.reshape(1, w), b_x.reshape(1, w),
            lam.reshape(1, w)]
    if reverse:
        in_specs += [chunk_spec, chunk_spec]
        args += [hf, lg]
    return pl.pallas_call(
        functools.partial(_lru_kernel, reverse=reverse),
        grid=(n,),
        in_specs=in_specs,
        out_specs=chunk_spec,
        out_shape=jax.ShapeDtypeStruct((b, s, w), BF16 if reverse else F32),
        scratch_shapes=[
            pltpu.VMEM((LRU_BLOCKS, b * LRU_PITCH, LRU_BLOCK_W), F32),
            pltpu.VMEM((LRU_BLOCKS, b * (tc + 2 * halo), LRU_BLOCK_W), F32),
            pltpu.VMEM((LRU_BLOCKS, b * tc, LRU_BLOCK_W), F32),
            pltpu.VMEM((LRU_BLOCKS, b, LRU_BLOCK_W), F32),
        ],
        compiler_params=_cparams(("arbitrary",)),
        name="lru_bwd" if reverse else "lru_fwd",
    )(*args)


def _merge_kernel(attn_ref, lru_ref, gates_ref, x_ref, mod_ref, wa_ref, wl_ref, wo_ref, o_ref, acat_s):
    for hd in range(N_Q_HEADS):
        acat_s[:, hd * HEAD_DIM:(hd + 1) * HEAD_DIM] = attn_ref[0, hd]
    a = jnp.dot(acat_s[...], wa_ref[...], preferred_element_type=F32)
    r = jnp.dot(lru_ref[0], wl_ref[...], preferred_element_type=F32)
    g = _sigmoid(gates_ref[0])
    merged = g[:, :D_MODEL] * a + g[:, D_MODEL:] * r
    y = jnp.dot(merged.astype(BF16), wo_ref[...], preferred_element_type=F32)
    o_ref[0] = x_ref[0] + mod_ref[0, 5:6, :] * y


def _merge_call(attn, lru, gates, x, mod, w_attn_o, w_lru_o, w_out):
    b, s, d = x.shape
    tm = MERGE_TM
    const2 = lambda bi, i: (0, 0)
    tok = lambda bi, i: (bi, i, 0)
    return pl.pallas_call(
        _merge_kernel,
        grid=(b, s // tm),
        in_specs=[
            pl.BlockSpec((1, N_Q_HEADS, tm, HEAD_DIM), lambda bi, i: (bi, 0, i, 0)),
            pl.BlockSpec((1, tm, LRU_WIDTH), tok),
            pl.BlockSpec((1, tm, 2 * d), tok),
            pl.BlockSpec((1, tm, d), tok),
            pl.BlockSpec((1, N_MOD, d), lambda bi, i: (bi, 0, 0)),
            pl.BlockSpec((ATTN_WIDTH, d), const2),
            pl.BlockSpec((LRU_WIDTH, d), const2),
            pl.BlockSpec((d, d), const2),
        ],
        out_specs=pl.BlockSpec((1, tm, d), tok),
        out_shape=jax.ShapeDtypeStruct((b, s, d), F32),
        scratch_shapes=[pltpu.VMEM((tm, ATTN_WIDTH), BF16)],
        compiler_params=_cparams(("arbitrary", "arbitrary")),
        name="merge_out",
    )(attn, lru, gates, x, mod, w_attn_o, w_lru_o, w_out)


def _rope_tables(s):
    rows = s // GRID_W
    row_ids = jnp.broadcast_to(jnp.arange(rows, dtype=F32)[:, None], (rows, GRID_W)).reshape(s)
    col_ids = jnp.broadcast_to(jnp.arange(GRID_W, dtype=F32)[None, :], (rows, GRID_W)).reshape(s)
    axis_dim = HEAD_DIM // 2
    inv_freq = ROPE_THETA ** (-jnp.arange(0, axis_dim, 2, dtype=F32) / axis_dim)
    ang = jnp.concatenate([row_ids[:, None] * inv_freq, col_ids[:, None] * inv_freq], axis=-1)
    cos, sin = jnp.cos(ang), jnp.sin(ang)
    return jnp.concatenate([cos, cos], axis=-1), jnp.concatenate([-sin, sin], axis=-1)


def _deinterleave_heads(w):
    lead = w.shape[:-1]
    n_heads = w.shape[-1] // HEAD_DIM
    w = w.reshape(*lead, n_heads, HEAD_DIM // 2, 2)
    return jnp.swapaxes(w, -1, -2).reshape(*lead, n_heads * HEAD_DIM)


def kernel(x, c, ada_w, ada_b, norm_g, ffn1_up, ffn1_down, w_in, q_norm_g, k_norm_g, conv_w, conv_b,
           lru_wa, lru_ba, lru_wx, lru_bx, lru_lambda, w_attn_o, w_lru_o, w_out, ffn2_up, ffn2_down, final_g):
    b, s, d = x.shape
    depth = ada_w.shape[0]
    mod = _ada_call(c, ada_w, ada_b).reshape(depth, b, N_MOD, d)
    cos, sin = _rope_tables(s)

    for l in range(depth):
        w_in_l = jnp.concatenate([_deinterleave_heads(w_in[l][:, :OFF_V]), w_in[l][:, OFF_V:]],
                                 axis=-1).astype(BF16)
        x = _ffn_call(x, mod[l], norm_g[l, 0], ffn1_up[l].astype(BF16), ffn1_down[l].astype(BF16), 0)

        q, k, v, lx, lg, gates = _proj_call(x, mod[l], norm_g[l, 1], w_in_l, _deinterleave_heads(q_norm_g[l]),
                                            _deinterleave_heads(k_norm_g[l]), cos, sin)
        attn = _attn_call(q, k, v)
        w_gate = jnp.concatenate([lru_wa[l], lru_wx[l]], axis=-1).astype(BF16)
        hf = _lru_call(lx, conv_w[l], conv_b[l], w_gate[0], lru_ba[l, 0], lru_bx[l, 0], lru_lambda[l, 0], False)
        lru = _lru_call(lx, conv_w[l], conv_b[l], w_gate[1], lru_ba[l, 1], lru_bx[l, 1], lru_lambda[l, 1], True,
                        hf=hf, lg=lg)
        x = _merge_call(attn, lru, gates, x, mod[l], w_attn_o[l].astype(BF16), w_lru_o[l].astype(BF16),
                        w_out[l].astype(BF16))
        x = _ffn_call(x, mod[l], norm_g[l, 2], ffn2_up[l].astype(BF16), ffn2_down[l].astype(BF16), 6,
                      final_g=final_g if l == depth - 1 else None)
    return x
```

```python
import functools
import math

import jax
import jax.numpy as jnp
from jax import lax
from jax.experimental import pallas as pl
from jax.experimental.pallas import tpu as pltpu

F32 = jnp.float32
BF16 = jnp.bfloat16

LANES = 128
SUBLANES = 8
VMEM_LIMIT_BYTES = 56 * 1024 * 1024

D_MODEL = 1024
HEAD_DIM = 128
N_Q_HEADS = 8
N_KV_HEADS = 2
GROUP = N_Q_HEADS // N_KV_HEADS
ATTN_WIDTH = N_Q_HEADS * HEAD_DIM
KV_WIDTH = N_KV_HEADS * HEAD_DIM
ROPE_THETA = 10000.0
GRID_W = 64
LRU_WIDTH = D_MODEL
LRU_BLOCKS = 8
LRU_BLOCK_W = LRU_WIDTH // LRU_BLOCKS
LRU_C = 8.0
D_FF = 2816
FFN_RES = 0.5
N_MOD = 9
EPS = 1e-6
Q_PRESCALE = (HEAD_DIM ** -0.5) * math.log2(math.e)

OFF_Q = 0
OFF_K = ATTN_WIDTH
OFF_V = OFF_K + KV_WIDTH
OFF_LX = OFF_V + KV_WIDTH
OFF_LG = OFF_LX + LRU_WIDTH
OFF_GATES = OFF_LG + LRU_WIDTH
IN_COLS = OFF_GATES + 2 * D_MODEL

ADA_TN = 2304
FFN_TM = 1024
FFN_TF = 512
FFN_CHUNKS = tuple((lo, min(lo + FFN_TF, D_FF)) for lo in range(0, D_FF, FFN_TF))
PROJ_TM = 512
ATT_TQ = 4096
ATT_M = 512
ATT_KC = 512
ATT_UNROLL = 8
LRU_TC = 128
LRU_HALO = SUBLANES
LRU_PITCH = LRU_TC + 2 * LRU_HALO + SUBLANES
LRU_SCAN_UNROLL = 8
MERGE_TM = 512


def _cparams(semantics):
    return pltpu.CompilerParams(dimension_semantics=semantics, vmem_limit_bytes=VMEM_LIMIT_BYTES)


def _sigmoid(x):
    return 1.0 / (1.0 + jnp.exp2(x * (-math.log2(math.e))))


def _rms_modulate(x, g, shift, scale):
    y = x * lax.rsqrt(jnp.mean(x * x, axis=-1, keepdims=True) + EPS) * g
    return y * (1.0 + scale) + shift


def _ada_kernel(c_ref, w_ref, b_ref, o_ref):
    c = c_ref[...]
    c_act = (c * _sigmoid(c)).astype(BF16)
    o_ref[0] = jnp.dot(c_act, w_ref[0].astype(BF16), preferred_element_type=F32) + b_ref[0]


def _ada_call(c, ada_w, ada_b):
    depth, d, n = ada_w.shape
    b = c.shape[0]
    return pl.pallas_call(
        _ada_kernel,
        grid=(depth, n // ADA_TN),
        in_specs=[
            pl.BlockSpec((b, d), lambda l, j: (0, 0)),
            pl.BlockSpec((1, d, ADA_TN), lambda l, j: (l, 0, j)),
            pl.BlockSpec((1, 1, ADA_TN), lambda l, j: (l, 0, j)),
        ],
        out_specs=pl.BlockSpec((1, b, ADA_TN), lambda l, j: (l, 0, j)),
        out_shape=jax.ShapeDtypeStruct((depth, b, n), F32),
        compiler_params=_cparams(("arbitrary", "arbitrary")),
        name="ada_mod",
    )(c, ada_w, ada_b.reshape(depth, 1, n))


def _ffn_kernel(*refs, mod_base, final_norm):
    if final_norm:
        x_ref, mod_ref, g_ref, wup_ref, wd_ref, fg_ref, o_ref, acc_s = refs
    else:
        x_ref, mod_ref, g_ref, wup_ref, wd_ref, o_ref, acc_s = refs
    shift = mod_ref[0, mod_base:mod_base + 1, :]
    scale = mod_ref[0, mod_base + 1:mod_base + 2, :]
    h = _rms_modulate(x_ref[0], g_ref[...], shift, scale).astype(BF16)
    for n, (lo, hi) in enumerate(FFN_CHUNKS):
        gate = jnp.dot(h, wup_ref[:, lo:hi], preferred_element_type=F32)
        up = jnp.dot(h, wup_ref[:, D_FF + lo:D_FF + hi], preferred_element_type=F32)
        act = (gate * _sigmoid(gate) * up).astype(BF16)
        part = jnp.dot(act, wd_ref[lo:hi, :], preferred_element_type=F32)
        if n == 0:
            acc_s[...] = part
        else:
            acc_s[...] += part
    res_gate = mod_ref[0, mod_base + 2:mod_base + 3, :]
    out = x_ref[0] + (FFN_RES * res_gate) * acc_s[...]
    if final_norm:
        out = out * lax.rsqrt(jnp.mean(out * out, axis=-1, keepdims=True) + EPS) * fg_ref[...]
    o_ref[0] = out


def _ffn_call(x, mod, g, w_up, w_down, mod_base, final_g=None):
    b, s, d = x.shape
    final_norm = final_g is not None
    const2 = lambda bi, i: (0, 0)
    in_specs = [
        pl.BlockSpec((1, FFN_TM, d), lambda bi, i: (bi, i, 0)),
        pl.BlockSpec((1, N_MOD, d), lambda bi, i: (bi, 0, 0)),
        pl.BlockSpec((1, d), const2),
        pl.BlockSpec(w_up.shape, const2, pipeline_mode=pl.Buffered(1)),
        pl.BlockSpec(w_down.shape, const2, pipeline_mode=pl.Buffered(1)),
    ]
    args = [x, mod, g.reshape(1, d), w_up, w_down]
    if final_norm:
        in_specs.append(pl.BlockSpec((1, d), const2))
        args.append(final_g.reshape(1, d))
    return pl.pallas_call(
        functools.partial(_ffn_kernel, mod_base=mod_base, final_norm=final_norm),
        grid=(b, s // FFN_TM),
        in_specs=in_specs,
        out_specs=pl.BlockSpec((1, FFN_TM, d), lambda bi, i: (bi, i, 0)),
        out_shape=jax.ShapeDtypeStruct((b, s, d), F32),
        scratch_shapes=[pltpu.VMEM((FFN_TM, d), F32)],
        compiler_params=_cparams(("arbitrary", "arbitrary")),
        name="ffn_final" if final_norm else "ffn",
    )(*args)


def _head_norm_rope(xh, g, cos, sin):
    y = xh * lax.rsqrt(jnp.mean(xh * xh, axis=-1, keepdims=True) + EPS) * g
    return y * cos + pltpu.roll(y, HEAD_DIM // 2, 1) * sin


def _proj_kernel(x_ref, mod_ref, g_ref, w_ref, qg_ref, kg_ref, cos_ref, sin_ref,
                 q_ref, k_ref, v_ref, lx_ref, lg_ref, gates_ref):
    h = _rms_modulate(x_ref[0], g_ref[...], mod_ref[0, 3:4, :], mod_ref[0, 4:5, :]).astype(BF16)
    cos = cos_ref[...]
    sin = sin_ref[...]

    q = jnp.dot(h, w_ref[:, OFF_Q:OFF_K], preferred_element_type=F32)
    for hd in range(N_Q_HEADS):
        qh = q[:, hd * HEAD_DIM:(hd + 1) * HEAD_DIM]
        q_ref[0, hd] = (_head_norm_rope(qh, qg_ref[...], cos, sin) * Q_PRESCALE).astype(BF16)

    kv = jnp.dot(h, w_ref[:, OFF_K:OFF_LX], preferred_element_type=F32)
    for hd in range(N_KV_HEADS):
        kh = kv[:, hd * HEAD_DIM:(hd + 1) * HEAD_DIM]
        k_ref[0, hd] = _head_norm_rope(kh, kg_ref[...], cos, sin).astype(BF16)
        v_ref[0, hd] = kv[:, KV_WIDTH + hd * HEAD_DIM:KV_WIDTH + (hd + 1) * HEAD_DIM].astype(BF16)

    lx_ref[0] = jnp.dot(h, w_ref[:, OFF_LX:OFF_LG], preferred_element_type=F32)
    lg_ref[0] = jnp.dot(h, w_ref[:, OFF_LG:OFF_GATES], preferred_element_type=F32)
    gates_ref[0] = jnp.dot(h, w_ref[:, OFF_GATES:IN_COLS], preferred_element_type=F32)


def _proj_call(x, mod, g, w_in, q_g, k_g, cos, sin):
    b, s, d = x.shape
    tm = PROJ_TM
    const2 = lambda bi, i: (0, 0)
    return pl.pallas_call(
        _proj_kernel,
        grid=(b, s // tm),
        in_specs=[
            pl.BlockSpec((1, tm, d), lambda bi, i: (bi, i, 0)),
            pl.BlockSpec((1, N_MOD, d), lambda bi, i: (bi, 0, 0)),
            pl.BlockSpec((1, d), const2),
            pl.BlockSpec((d, IN_COLS), const2, pipeline_mode=pl.Buffered(1)),
            pl.BlockSpec((1, HEAD_DIM), const2),
            pl.BlockSpec((1, HEAD_DIM), const2),
            pl.BlockSpec((tm, HEAD_DIM), lambda bi, i: (i, 0)),
            pl.BlockSpec((tm, HEAD_DIM), lambda bi, i: (i, 0)),
        ],
        out_specs=[
            pl.BlockSpec((1, N_Q_HEADS, tm, HEAD_DIM), lambda bi, i: (bi, 0, i, 0)),
            pl.BlockSpec((1, N_KV_HEADS, tm, HEAD_DIM), lambda bi, i: (bi, 0, i, 0)),
            pl.BlockSpec((1, N_KV_HEADS, tm, HEAD_DIM), lambda bi, i: (bi, 0, i, 0)),
            pl.BlockSpec((1, tm, LRU_WIDTH), lambda bi, i: (bi, i, 0)),
            pl.BlockSpec((1, tm, LRU_WIDTH), lambda bi, i: (bi, i, 0)),
            pl.BlockSpec((1, tm, 2 * d), lambda bi, i: (bi, i, 0)),
        ],
        out_shape=[
            jax.ShapeDtypeStruct((b, N_Q_HEADS, s, HEAD_DIM), BF16),
            jax.ShapeDtypeStruct((b, N_KV_HEADS, s, HEAD_DIM), BF16),
            jax.ShapeDtypeStruct((b, N_KV_HEADS, s, HEAD_DIM), BF16),
            jax.ShapeDtypeStruct((b, s, LRU_WIDTH), F32),
            jax.ShapeDtypeStruct((b, s, LRU_WIDTH), F32),
            jax.ShapeDtypeStruct((b, s, 2 * d), F32),
        ],
        compiler_params=_cparams(("arbitrary", "arbitrary")),
        name="in_proj",
    )(x, mod, g.reshape(1, d), w_in, q_g.reshape(1, HEAD_DIM), k_g.reshape(1, HEAD_DIM), cos, sin)


def _attn_kernel(q_ref, k_ref, v_ref, o_ref, s_scr, m_scr, vext_s):
    n_kc = k_ref.shape[2] // ATT_KC
    row_blocks = ATT_TQ // ATT_M
    n_items = GROUP * row_blocks

    vext_s[:, :HEAD_DIM] = v_ref[0, 0]
    vext_s[:, HEAD_DIM:] = jnp.ones((vext_s.shape[0], HEAD_DIM), BF16)

    def item_rows(item):
        g = item // row_blocks
        r0 = pl.multiple_of((item % row_blocks) * ATT_M, ATT_M)
        return g, pl.ds(r0, ATT_M)

    def scores(item, slot):
        g, rows = item_rows(item)
        q = q_ref[0, g, rows, :]
        m = None
        for c in range(n_kc):
            kc = k_ref[0, 0, c * ATT_KC:(c + 1) * ATT_KC, :]
            s = lax.dot_general(q, kc, (((1,), (1,)), ((), ())), preferred_element_type=F32)
            s_scr[slot, :, c * ATT_KC:(c + 1) * ATT_KC] = s
            mc = jnp.max(s, axis=-1, keepdims=True)
            m = mc if m is None else jnp.maximum(m, mc)
        m_scr[slot] = m

    def finish(item, slot):
        g, rows = item_rows(item)
        m = m_scr[slot]
        acc = jnp.zeros((ATT_M, 2 * HEAD_DIM), F32)
        for c in range(n_kc):
            p = jnp.exp2(s_scr[slot, :, c * ATT_KC:(c + 1) * ATT_KC] - m)
            vc = vext_s[c * ATT_KC:(c + 1) * ATT_KC, :]
            acc = acc + jnp.dot(p.astype(BF16), vc, preferred_element_type=F32)
        o_ref[0, g, rows, :] = (acc[:, :HEAD_DIM] / acc[:, HEAD_DIM:]).astype(BF16)

    scores(jnp.int32(0), 0)

    def group_body(j, carry):
        first = ATT_UNROLL * j
        for u in range(ATT_UNROLL):
            scores(jnp.minimum(first + u + 1, n_items - 1), (u + 1) % 2)
            finish(first + u, u % 2)
        return carry

    lax.fori_loop(0, n_items // ATT_UNROLL, group_body, 0)


def _attn_call(q, k, v):
    b, _, s, _ = q.shape
    return pl.pallas_call(
        _attn_kernel,
        grid=(b, N_KV_HEADS, s // ATT_TQ),
        in_specs=[
            pl.BlockSpec((1, GROUP, ATT_TQ, HEAD_DIM), lambda bi, kh, i: (bi, kh, i, 0)),
            pl.BlockSpec((1, 1, s, HEAD_DIM), lambda bi, kh, i: (bi, kh, 0, 0)),
            pl.BlockSpec((1, 1, s, HEAD_DIM), lambda bi, kh, i: (bi, kh, 0, 0)),
        ],
        out_specs=pl.BlockSpec((1, GROUP, ATT_TQ, HEAD_DIM), lambda bi, kh, i: (bi, kh, i, 0)),
        out_shape=jax.ShapeDtypeStruct((b, N_Q_HEADS, s, HEAD_DIM), BF16),
        scratch_shapes=[pltpu.VMEM((2, ATT_M, s), F32), pltpu.VMEM((2, ATT_M, 1), F32),
                        pltpu.VMEM((s, 2 * HEAD_DIM), BF16)],
        compiler_params=_cparams(("arbitrary", "arbitrary", "arbitrary")),
        name="attention",
    )(q, k, v)


def _gelu_tanh(x):
    c = math.sqrt(2.0 / math.pi)
    half_x = 0.5 * x
    return half_x + half_x * jnp.tanh(x * (c + (0.044715 * c) * (x * x)))


def _lru_gates(xb, k, wg_ref, ba_ref, bx_ref, lam_ref):
    cols = slice(k * LRU_BLOCK_W, (k + 1) * LRU_BLOCK_W)
    pre = jnp.dot(xb.astype(BF16), wg_ref[k], preferred_element_type=F32)
    r = _sigmoid(pre[:, :LRU_BLOCK_W] + ba_ref[:, cols])
    ig = _sigmoid(pre[:, LRU_BLOCK_W:] + bx_ref[:, cols])
    z = -lam_ref[:, cols]
    softplus = jnp.maximum(z, 0.0) + jnp.log1p(jnp.exp(-jnp.abs(z)))
    a = jnp.exp2(r * ((-LRU_C * math.log2(math.e)) * softplus))
    one_m_a2 = 1.0 - a * a
    root = jnp.where(one_m_a2 > 0.0, one_m_a2 * lax.rsqrt(one_m_a2), 0.0)
    return a, root * (ig * xb)


def _lru_scan(a_s, u_s, hc_s, nb, reverse):
    n_groups = LRU_TC // LRU_SCAN_UNROLL
    group_rows = LRU_SCAN_UNROLL * nb

    def group(j, hs):
        base = pl.multiple_of(((n_groups - 1 - j) if reverse else j) * group_rows, group_rows)
        hs = list(hs)
        for t in (reversed(range(LRU_SCAN_UNROLL)) if reverse else range(LRU_SCAN_UNROLL)):
            rows = pl.ds(base + t * nb, nb)
            for k in range(LRU_BLOCKS):
                hs[k] = a_s[k, rows, :] * hs[k] + u_s[k, rows, :]
                u_s[k, rows, :] = hs[k]
        return tuple(hs)

    hs = lax.fori_loop(0, n_groups, group, tuple(hc_s[k] for k in range(LRU_BLOCKS)))
    for k in range(LRU_BLOCKS):
        hc_s[k] = hs[k]


def _lru_fwd_kernel(x_ref, prev_ref, next_ref, cw_ref, cb_ref, wg_ref, ba_ref, bx_ref, lam_ref,
                    hf_ref, xc_ref, bm_s, xt_s, u_s, hc_s):
    nb = x_ref.shape[0]
    tc, pitch, halo = LRU_TC, LRU_PITCH, LRU_HALO
    ext = tc + 2 * halo
    chunk = pl.program_id(0)
    n = pl.num_programs(0)

    @pl.when(chunk == 0)
    def _():
        hc_s[...] = jnp.zeros_like(hc_s)

    for k in range(LRU_BLOCKS):
        cols = slice(k * LRU_BLOCK_W, (k + 1) * LRU_BLOCK_W)
        for b in range(nb):
            base = b * pitch
            bm_s[k, base:base + halo, :] = jnp.where(chunk == 0, 0.0, prev_ref[b, :, cols])
            bm_s[k, base + halo:base + halo + tc, :] = x_ref[b, :, cols]
            bm_s[k, base + halo + tc:base + ext, :] = jnp.where(chunk == n - 1, 0.0, next_ref[b, :, cols])

    def to_time_major(tau, carry):
        dst = pl.ds(pl.multiple_of(tau * nb, nb), nb)
        for k in range(LRU_BLOCKS):
            xt_s[k, dst, :] = bm_s[k, pl.ds(tau, nb, stride=pitch), :]
        return carry

    lax.fori_loop(0, ext, to_time_major, 0, unroll=8)

    for k in range(LRU_BLOCKS):
        cols = slice(k * LRU_BLOCK_W, (k + 1) * LRU_BLOCK_W)
        xb = cb_ref[:, cols]
        for j in range(cw_ref.shape[0]):
            r0 = (halo - 2 + j) * nb
            xb = xb + xt_s[k, r0:r0 + tc * nb, :] * cw_ref[j:j + 1, cols]
        xc_ref[0, k] = xb
        a, u = _lru_gates(xb, k, wg_ref, ba_ref, bx_ref, lam_ref)
        bm_s[k, 0:tc * nb, :] = a
        u_s[k] = u

    _lru_scan(bm_s, u_s, hc_s, nb, reverse=False)

    for k in range(LRU_BLOCKS):
        cols = slice(k * LRU_BLOCK_W, (k + 1) * LRU_BLOCK_W)
        for b in range(nb):
            hf_ref[b, :, cols] = u_s[k, pl.ds(b, tc, stride=nb), :]


def _lru_bwd_kernel(xc_ref, wg_ref, ba_ref, bx_ref, lam_ref, hf_ref, lg_ref, o_ref, a_s, u_s, hc_s):
    nb = hf_ref.shape[0]

    @pl.when(pl.program_id(0) == 0)
    def _():
        hc_s[...] = jnp.zeros_like(hc_s)

    for k in range(LRU_BLOCKS):
        a, u = _lru_gates(xc_ref[0, k], k, wg_ref, ba_ref, bx_ref, lam_ref)
        a_s[k] = a
        u_s[k] = u

    _lru_scan(a_s, u_s, hc_s, nb, reverse=True)

    for k in range(LRU_BLOCKS):
        cols = slice(k * LRU_BLOCK_W, (k + 1) * LRU_BLOCK_W)
        for b in range(nb):
            hk = u_s[k, pl.ds(b, LRU_TC, stride=nb), :]
            o_ref[b, :, cols] = ((hf_ref[b, :, cols] + hk) * _gelu_tanh(lg_ref[b, :, cols])).astype(BF16)


def _lru_call(lx, lg, conv_w, conv_b, w_gate, b_a, b_x, lam):
    b, s, w = lx.shape
    tc, halo = LRU_TC, LRU_HALO
    n = s // tc
    hb_per_chunk = tc // halo
    last_hb = s // halo - 1
    const2 = lambda i: (0, 0)
    const3 = lambda i: (0, 0, 0)
    slab_rows = b * tc
    vec = lambda a, d: a[d].reshape(1, w)
    gate_specs = [pl.BlockSpec(w_gate.shape[1:], const3)] + [pl.BlockSpec((1, w), const2)] * 3
    slab = lambda rows: pltpu.VMEM((LRU_BLOCKS, rows, LRU_BLOCK_W), F32)

    hf, xc = pl.pallas_call(
        _lru_fwd_kernel,
        grid=(n,),
        in_specs=[
            pl.BlockSpec((b, tc, w), lambda i: (0, i, 0)),
            pl.BlockSpec((b, halo, w), lambda i: (0, jnp.maximum(i * hb_per_chunk - 1, 0), 0)),
            pl.BlockSpec((b, halo, w), lambda i: (0, jnp.minimum((i + 1) * hb_per_chunk, last_hb), 0)),
            pl.BlockSpec(conv_w.shape, const2),
            pl.BlockSpec((1, w), const2),
        ] + gate_specs,
        out_specs=[
            pl.BlockSpec((b, tc, w), lambda i: (0, i, 0)),
            pl.BlockSpec((1, LRU_BLOCKS, slab_rows, LRU_BLOCK_W), lambda i: (i, 0, 0, 0)),
        ],
        out_shape=[
            jax.ShapeDtypeStruct((b, s, w), F32),
            jax.ShapeDtypeStruct((n, LRU_BLOCKS, slab_rows, LRU_BLOCK_W), F32),
        ],
        scratch_shapes=[slab(b * LRU_PITCH), slab(b * (tc + 2 * halo)), slab(slab_rows), slab(b)],
        compiler_params=_cparams(("arbitrary",)),
        name="lru_fwd",
    )(lx, lx, lx, conv_w, conv_b.reshape(1, w), w_gate[0], vec(b_a, 0), vec(b_x, 0), vec(lam, 0))

    rev_chunk = pl.BlockSpec((b, tc, w), lambda i: (0, n - 1 - i, 0))
    return pl.pallas_call(
        _lru_bwd_kernel,
        grid=(n,),
        in_specs=[pl.BlockSpec((1, LRU_BLOCKS, slab_rows, LRU_BLOCK_W), lambda i: (n - 1 - i, 0, 0, 0))]
        + gate_specs + [rev_chunk, rev_chunk],
        out_specs=rev_chunk,
        out_shape=jax.ShapeDtypeStruct((b, s, w), BF16),
        scratch_shapes=[slab(slab_rows), slab(slab_rows), slab(b)],
        compiler_params=_cparams(("arbitrary",)),
        name="lru_bwd",
    )(xc, w_gate[1], vec(b_a, 1), vec(b_x, 1), vec(lam, 1), hf, lg)


def _merge_kernel(attn_ref, lru_ref, gates_ref, x_ref, mod_ref, wa_ref, wl_ref, wo_ref, o_ref, acat_s):
    for hd in range(N_Q_HEADS):
        acat_s[:, hd * HEAD_DIM:(hd + 1) * HEAD_DIM] = attn_ref[0, hd]
    a = jnp.dot(acat_s[...], wa_ref[...], preferred_element_type=F32)
    r = jnp.dot(lru_ref[0], wl_ref[...], preferred_element_type=F32)
    g = _sigmoid(gates_ref[0])
    merged = g[:, :D_MODEL] * a + g[:, D_MODEL:] * r
    y = jnp.dot(merged.astype(BF16), wo_ref[...], preferred_element_type=F32)
    o_ref[0] = x_ref[0] + mod_ref[0, 5:6, :] * y


def _merge_call(attn, lru, gates, x, mod, w_attn_o, w_lru_o, w_out):
    b, s, d = x.shape
    tm = MERGE_TM
    const2 = lambda bi, i: (0, 0)
    tok = lambda bi, i: (bi, i, 0)
    return pl.pallas_call(
        _merge_kernel,
        grid=(b, s // tm),
        in_specs=[
            pl.BlockSpec((1, N_Q_HEADS, tm, HEAD_DIM), lambda bi, i: (bi, 0, i, 0)),
            pl.BlockSpec((1, tm, LRU_WIDTH), tok),
            pl.BlockSpec((1, tm, 2 * d), tok),
            pl.BlockSpec((1, tm, d), tok),
            pl.BlockSpec((1, N_MOD, d), lambda bi, i: (bi, 0, 0)),
            pl.BlockSpec((ATTN_WIDTH, d), const2),
            pl.BlockSpec((LRU_WIDTH, d), const2),
            pl.BlockSpec((d, d), const2),
        ],
        out_specs=pl.BlockSpec((1, tm, d), tok),
        out_shape=jax.ShapeDtypeStruct((b, s, d), F32),
        scratch_shapes=[pltpu.VMEM((tm, ATTN_WIDTH), BF16)],
        compiler_params=_cparams(("arbitrary", "arbitrary")),
        name="merge_out",
    )(attn, lru, gates, x, mod, w_attn_o, w_lru_o, w_out)


def _rope_tables(s):
    rows = s // GRID_W
    row_ids = jnp.broadcast_to(jnp.arange(rows, dtype=F32)[:, None], (rows, GRID_W)).reshape(s)
    col_ids = jnp.broadcast_to(jnp.arange(GRID_W, dtype=F32)[None, :], (rows, GRID_W)).reshape(s)
    axis_dim = HEAD_DIM // 2
    inv_freq = ROPE_THETA ** (-jnp.arange(0, axis_dim, 2, dtype=F32) / axis_dim)
    ang = jnp.concatenate([row_ids[:, None] * inv_freq, col_ids[:, None] * inv_freq], axis=-1)
    cos, sin = jnp.cos(ang), jnp.sin(ang)
    return jnp.concatenate([cos, cos], axis=-1), jnp.concatenate([-sin, sin], axis=-1)


def _deinterleave_heads(w):
    lead = w.shape[:-1]
    n_heads = w.shape[-1] // HEAD_DIM
    w = w.reshape(*lead, n_heads, HEAD_DIM // 2, 2)
    return jnp.swapaxes(w, -1, -2).reshape(*lead, n_heads * HEAD_DIM)


def kernel(x, c, ada_w, ada_b, norm_g, ffn1_up, ffn1_down, w_in, q_norm_g, k_norm_g, conv_w, conv_b,
           lru_wa, lru_ba, lru_wx, lru_bx, lru_lambda, w_attn_o, w_lru_o, w_out, ffn2_up, ffn2_down, final_g):
    b, s, d = x.shape
    depth = ada_w.shape[0]
    mod = _ada_call(c, ada_w, ada_b).reshape(depth, b, N_MOD, d)
    cos, sin = _rope_tables(s)

    for l in range(depth):
        w_in_l = jnp.concatenate([_deinterleave_heads(w_in[l][:, :OFF_V]), w_in[l][:, OFF_V:]],
                                 axis=-1).astype(BF16)
        x = _ffn_call(x, mod[l], norm_g[l, 0], ffn1_up[l].astype(BF16), ffn1_down[l].astype(BF16), 0)

        q, k, v, lx, lg, gates = _proj_call(x, mod[l], norm_g[l, 1], w_in_l, _deinterleave_heads(q_norm_g[l]),
                                            _deinterleave_heads(k_norm_g[l]), cos, sin)
        attn = _attn_call(q, k, v)
        w_gate = jnp.concatenate([lru_wa[l], lru_wx[l]], axis=-1).astype(BF16)
        lru = _lru_call(lx, lg, conv_w[l], conv_b[l], w_gate, lru_ba[l], lru_bx[l], lru_lambda[l])
        x = _merge_call(attn, lru, gates, x, mod[l], w_attn_o[l].astype(BF16), w_lru_o[l].astype(BF16),
                        w_out[l].astype(BF16))
        x = _ffn_call(x, mod[l], norm_g[l, 2], ffn2_up[l].astype(BF16), ffn2_down[l].astype(BF16), 6,
                      final_g=final_g if l == depth - 1 else None)
    return x
```

```python
import functools
import math

import jax
import jax.numpy as jnp
from jax import lax
from jax.experimental import pallas as pl
from jax.experimental.pallas import tpu as pltpu

F32 = jnp.float32
BF16 = jnp.bfloat16

SUBLANES = 8
VMEM_LIMIT_BYTES = 56 * 1024 * 1024

D_MODEL = 1024
HEAD_DIM = 128
N_Q_HEADS = 8
N_KV_HEADS = 2
GROUP = N_Q_HEADS // N_KV_HEADS
ATTN_WIDTH = N_Q_HEADS * HEAD_DIM
KV_WIDTH = N_KV_HEADS * HEAD_DIM
ROPE_THETA = 10000.0
GRID_W = 64
LRU_WIDTH = D_MODEL
LRU_BLOCKS = 8
LRU_BLOCK_W = LRU_WIDTH // LRU_BLOCKS
LRU_C = 8.0
D_FF = 2816
FFN_RES = 0.5
N_MOD = 9
EPS = 1e-6
Q_PRESCALE = (HEAD_DIM ** -0.5) * math.log2(math.e)

OFF_Q = 0
OFF_K = ATTN_WIDTH
OFF_V = OFF_K + KV_WIDTH
OFF_LX = OFF_V + KV_WIDTH
OFF_LG = OFF_LX + LRU_WIDTH
OFF_GATES = OFF_LG + LRU_WIDTH
IN_COLS = OFF_GATES + 2 * D_MODEL

ADA_TN = 2304
FFN_TM = 1024
FFN_TF = 512
FFN_CHUNKS = tuple((lo, min(lo + FFN_TF, D_FF)) for lo in range(0, D_FF, FFN_TF))
PROJ_TM = 512
ATT_TQ = 4096
ATT_M = 512
ATT_KC = 512
ATT_UNROLL = 8
LRU_TC = 128
LRU_HALO = SUBLANES
LRU_PITCH = LRU_TC + 2 * LRU_HALO + SUBLANES
LRU_SCAN_UNROLL = 8
MERGE_TM = 512


def _cparams(semantics):
    return pltpu.CompilerParams(dimension_semantics=semantics, vmem_limit_bytes=VMEM_LIMIT_BYTES)


def _sigmoid(x):
    return 1.0 / (1.0 + jnp.exp2(x * (-math.log2(math.e))))


def _rms_modulate(x, g, shift, scale):
    y = x * lax.rsqrt(jnp.mean(x * x, axis=-1, keepdims=True) + EPS) * g
    return y * (1.0 + scale) + shift


def _ada_kernel(c_ref, w_ref, b_ref, o_ref):
    c = c_ref[...]
    c_act = (c * _sigmoid(c)).astype(BF16)
    o_ref[0] = jnp.dot(c_act, w_ref[0].astype(BF16), preferred_element_type=F32) + b_ref[0]


def _ada_call(c, ada_w, ada_b):
    depth, d, n = ada_w.shape
    b = c.shape[0]
    return pl.pallas_call(
        _ada_kernel,
        grid=(depth, n // ADA_TN),
        in_specs=[
            pl.BlockSpec((b, d), lambda l, j: (0, 0)),
            pl.BlockSpec((1, d, ADA_TN), lambda l, j: (l, 0, j)),
            pl.BlockSpec((1, 1, ADA_TN), lambda l, j: (l, 0, j)),
        ],
        out_specs=pl.BlockSpec((1, b, ADA_TN), lambda l, j: (l, 0, j)),
        out_shape=jax.ShapeDtypeStruct((depth, b, n), F32),
        compiler_params=_cparams(("arbitrary", "arbitrary")),
        name="ada_mod",
    )(c, ada_w, ada_b.reshape(depth, 1, n))


def _ffn_kernel(*refs, mod_base, final_norm):
    if final_norm:
        x_ref, mod_ref, g_ref, wup_ref, wd_ref, fg_ref, o_ref, acc_s = refs
    else:
        x_ref, mod_ref, g_ref, wup_ref, wd_ref, o_ref, acc_s = refs
    shift = mod_ref[0, mod_base:mod_base + 1, :]
    scale = mod_ref[0, mod_base + 1:mod_base + 2, :]
    h = _rms_modulate(x_ref[0], g_ref[...], shift, scale).astype(BF16)
    for n, (lo, hi) in enumerate(FFN_CHUNKS):
        gate = jnp.dot(h, wup_ref[:, lo:hi], preferred_element_type=F32)
        up = jnp.dot(h, wup_ref[:, D_FF + lo:D_FF + hi], preferred_element_type=F32)
        act = (gate * _sigmoid(gate) * up).astype(BF16)
        part = jnp.dot(act, wd_ref[lo:hi, :], preferred_element_type=F32)
        if n == 0:
            acc_s[...] = part
        else:
            acc_s[...] += part
    res_gate = mod_ref[0, mod_base + 2:mod_base + 3, :]
    out = x_ref[0] + (FFN_RES * res_gate) * acc_s[...]
    if final_norm:
        out = out * lax.rsqrt(jnp.mean(out * out, axis=-1, keepdims=True) + EPS) * fg_ref[...]
    o_ref[0] = out


def _layer_weight_spec(w, l):
    return pl.BlockSpec((None,) + w.shape[1:], lambda *_: (l, 0, 0), pipeline_mode=pl.Buffered(1))


def _ffn_call(x, mod, g, w_up, w_down, l, mod_base, final_g=None):
    b, s, d = x.shape
    final_norm = final_g is not None
    const2 = lambda bi, i: (0, 0)
    in_specs = [
        pl.BlockSpec((1, FFN_TM, d), lambda bi, i: (bi, i, 0)),
        pl.BlockSpec((1, N_MOD, d), lambda bi, i: (bi, 0, 0)),
        pl.BlockSpec((1, d), const2),
        _layer_weight_spec(w_up, l),
        _layer_weight_spec(w_down, l),
    ]
    args = [x, mod, g.reshape(1, d), w_up, w_down]
    if final_norm:
        in_specs.append(pl.BlockSpec((1, d), const2))
        args.append(final_g.reshape(1, d))
    return pl.pallas_call(
        functools.partial(_ffn_kernel, mod_base=mod_base, final_norm=final_norm),
        grid=(b, s // FFN_TM),
        in_specs=in_specs,
        out_specs=pl.BlockSpec((1, FFN_TM, d), lambda bi, i: (bi, i, 0)),
        out_shape=jax.ShapeDtypeStruct((b, s, d), F32),
        scratch_shapes=[pltpu.VMEM((FFN_TM, d), F32)],
        compiler_params=_cparams(("arbitrary", "arbitrary")),
        name="ffn_final" if final_norm else "ffn",
    )(*args)


def _head_norm_rope(xh, g, cos, sin):
    y = xh * lax.rsqrt(jnp.mean(xh * xh, axis=-1, keepdims=True) + EPS) * g
    return y * cos + pltpu.roll(y, HEAD_DIM // 2, 1) * sin


def _proj_kernel(x_ref, mod_ref, g_ref, wqk_ref, wr_ref, qg_ref, kg_ref, cos_ref, sin_ref,
                 q_ref, k_ref, v_ref, lx_ref, lg_ref, gates_ref):
    h = _rms_modulate(x_ref[0], g_ref[...], mod_ref[0, 3:4, :], mod_ref[0, 4:5, :]).astype(BF16)
    cos = cos_ref[...]
    sin = sin_ref[...]

    q = jnp.dot(h, wqk_ref[:, OFF_Q:OFF_K], preferred_element_type=F32)
    for hd in range(N_Q_HEADS):
        qh = q[:, hd * HEAD_DIM:(hd + 1) * HEAD_DIM]
        q_ref[0, hd] = (_head_norm_rope(qh, qg_ref[...], cos, sin) * Q_PRESCALE).astype(BF16)

    k = jnp.dot(h, wqk_ref[:, OFF_K:OFF_V], preferred_element_type=F32)
    v = jnp.dot(h, wr_ref[:, 0:KV_WIDTH], preferred_element_type=F32)
    for hd in range(N_KV_HEADS):
        head = slice(hd * HEAD_DIM, (hd + 1) * HEAD_DIM)
        k_ref[0, hd] = _head_norm_rope(k[:, head], kg_ref[...], cos, sin).astype(BF16)
        v_ref[0, hd] = v[:, head].astype(BF16)

    lx_ref[0] = jnp.dot(h, wr_ref[:, OFF_LX - OFF_V:OFF_LG - OFF_V], preferred_element_type=F32)
    lg_ref[0] = jnp.dot(h, wr_ref[:, OFF_LG - OFF_V:OFF_GATES - OFF_V], preferred_element_type=F32)
    gates_ref[0] = jnp.dot(h, wr_ref[:, OFF_GATES - OFF_V:IN_COLS - OFF_V], preferred_element_type=F32)


def _proj_call(x, mod, g, w_qk, w_rest, l, q_g, k_g, cos, sin):
    b, s, d = x.shape
    tm = PROJ_TM
    const2 = lambda bi, i: (0, 0)
    return pl.pallas_call(
        _proj_kernel,
        grid=(b, s // tm),
        in_specs=[
            pl.BlockSpec((1, tm, d), lambda bi, i: (bi, i, 0)),
            pl.BlockSpec((1, N_MOD, d), lambda bi, i: (bi, 0, 0)),
            pl.BlockSpec((1, d), const2),
            _layer_weight_spec(w_qk, l),
            _layer_weight_spec(w_rest, l),
            pl.BlockSpec((1, HEAD_DIM), const2),
            pl.BlockSpec((1, HEAD_DIM), const2),
            pl.BlockSpec((tm, HEAD_DIM), lambda bi, i: (i, 0)),
            pl.BlockSpec((tm, HEAD_DIM), lambda bi, i: (i, 0)),
        ],
        out_specs=[
            pl.BlockSpec((1, N_Q_HEADS, tm, HEAD_DIM), lambda bi, i: (bi, 0, i, 0)),
            pl.BlockSpec((1, N_KV_HEADS, tm, HEAD_DIM), lambda bi, i: (bi, 0, i, 0)),
            pl.BlockSpec((1, N_KV_HEADS, tm, HEAD_DIM), lambda bi, i: (bi, 0, i, 0)),
            pl.BlockSpec((1, tm, LRU_WIDTH), lambda bi, i: (bi, i, 0)),
            pl.BlockSpec((1, tm, LRU_WIDTH), lambda bi, i: (bi, i, 0)),
            pl.BlockSpec((1, tm, 2 * d), lambda bi, i: (bi, i, 0)),
        ],
        out_shape=[
            jax.ShapeDtypeStruct((b, N_Q_HEADS, s, HEAD_DIM), BF16),
            jax.ShapeDtypeStruct((b, N_KV_HEADS, s, HEAD_DIM), BF16),
            jax.ShapeDtypeStruct((b, N_KV_HEADS, s, HEAD_DIM), BF16),
            jax.ShapeDtypeStruct((b, s, LRU_WIDTH), F32),
            jax.ShapeDtypeStruct((b, s, LRU_WIDTH), F32),
            jax.ShapeDtypeStruct((b, s, 2 * d), F32),
        ],
        compiler_params=_cparams(("arbitrary", "arbitrary")),
        name="in_proj",
    )(x, mod, g.reshape(1, d), w_qk, w_rest, q_g.reshape(1, HEAD_DIM), k_g.reshape(1, HEAD_DIM), cos, sin)


def _attn_kernel(q_ref, k_ref, v_ref, o_ref, s_scr, m_scr, vext_s):
    n_kc = k_ref.shape[2] // ATT_KC
    row_blocks = ATT_TQ // ATT_M
    n_items = GROUP * row_blocks

    vext_s[:, :HEAD_DIM] = v_ref[0, 0]
    vext_s[:, HEAD_DIM:] = jnp.ones((vext_s.shape[0], HEAD_DIM), BF16)

    def item_rows(item):
        g = item // row_blocks
        r0 = pl.multiple_of((item % row_blocks) * ATT_M, ATT_M)
        return g, pl.ds(r0, ATT_M)

    def scores(item, slot):
        g, rows = item_rows(item)
        q = q_ref[0, g, rows, :]
        m = None
        for c in range(n_kc):
            kc = k_ref[0, 0, c * ATT_KC:(c + 1) * ATT_KC, :]
            s = lax.dot_general(q, kc, (((1,), (1,)), ((), ())), preferred_element_type=F32)
            s_scr[slot, :, c * ATT_KC:(c + 1) * ATT_KC] = s
            mc = jnp.max(s, axis=-1, keepdims=True)
            m = mc if m is None else jnp.maximum(m, mc)
        m_scr[slot] = m

    def finish(item, slot):
        g, rows = item_rows(item)
        m = m_scr[slot]
        acc = jnp.zeros((ATT_M, 2 * HEAD_DIM), F32)
        for c in range(n_kc):
            p = jnp.exp2(s_scr[slot, :, c * ATT_KC:(c + 1) * ATT_KC] - m)
            vc = vext_s[c * ATT_KC:(c + 1) * ATT_KC, :]
            acc = acc + jnp.dot(p.astype(BF16), vc, preferred_element_type=F32)
        o_ref[0, g, rows, :] = (acc[:, :HEAD_DIM] / acc[:, HEAD_DIM:]).astype(BF16)

    scores(jnp.int32(0), 0)

    def group_body(j, carry):
        first = ATT_UNROLL * j
        for u in range(ATT_UNROLL):
            scores(jnp.minimum(first + u + 1, n_items - 1), (u + 1) % 2)
            finish(first + u, u % 2)
        return carry

    lax.fori_loop(0, n_items // ATT_UNROLL, group_body, 0)


def _attn_call(q, k, v):
    b, _, s, _ = q.shape
    return pl.pallas_call(
        _attn_kernel,
        grid=(b, N_KV_HEADS, s // ATT_TQ),
        in_specs=[
            pl.BlockSpec((1, GROUP, ATT_TQ, HEAD_DIM), lambda bi, kh, i: (bi, kh, i, 0)),
            pl.BlockSpec((1, 1, s, HEAD_DIM), lambda bi, kh, i: (bi, kh, 0, 0)),
            pl.BlockSpec((1, 1, s, HEAD_DIM), lambda bi, kh, i: (bi, kh, 0, 0)),
        ],
        out_specs=pl.BlockSpec((1, GROUP, ATT_TQ, HEAD_DIM), lambda bi, kh, i: (bi, kh, i, 0)),
        out_shape=jax.ShapeDtypeStruct((b, N_Q_HEADS, s, HEAD_DIM), BF16),
        scratch_shapes=[pltpu.VMEM((2, ATT_M, s), F32), pltpu.VMEM((2, ATT_M, 1), F32),
                        pltpu.VMEM((s, 2 * HEAD_DIM), BF16)],
        compiler_params=_cparams(("arbitrary", "arbitrary", "arbitrary")),
        name="attention",
    )(q, k, v)


def _gelu_tanh(x):
    c = math.sqrt(2.0 / math.pi)
    half_x = 0.5 * x
    return half_x + half_x * jnp.tanh(x * (c + (0.044715 * c) * (x * x)))


def _lru_gates(xb, k, wg_ref, ba_ref, bx_ref, lam_ref):
    cols = slice(k * LRU_BLOCK_W, (k + 1) * LRU_BLOCK_W)
    pre = jnp.dot(xb.astype(BF16), wg_ref[k], preferred_element_type=F32)
    r = _sigmoid(pre[:, :LRU_BLOCK_W] + ba_ref[:, cols])
    ig = _sigmoid(pre[:, LRU_BLOCK_W:] + bx_ref[:, cols])
    z = -lam_ref[:, cols]
    softplus = jnp.maximum(z, 0.0) + jnp.log1p(jnp.exp(-jnp.abs(z)))
    a = jnp.exp2(r * ((-LRU_C * math.log2(math.e)) * softplus))
    one_m_a2 = 1.0 - a * a
    root = jnp.where(one_m_a2 > 0.0, one_m_a2 * lax.rsqrt(one_m_a2), 0.0)
    return a, root * (ig * xb)


def _lru_scan(a_s, u_s, hc_s, nb, reverse):
    n_groups = LRU_TC // LRU_SCAN_UNROLL
    group_rows = LRU_SCAN_UNROLL * nb

    def group(j, hs):
        base = pl.multiple_of(((n_groups - 1 - j) if reverse else j) * group_rows, group_rows)
        hs = list(hs)
        for t in (reversed(range(LRU_SCAN_UNROLL)) if reverse else range(LRU_SCAN_UNROLL)):
            rows = pl.ds(base + t * nb, nb)
            for k in range(LRU_BLOCKS):
                hs[k] = a_s[k, rows, :] * hs[k] + u_s[k, rows, :]
                u_s[k, rows, :] = hs[k]
        return tuple(hs)

    hs = lax.fori_loop(0, n_groups, group, tuple(hc_s[k] for k in range(LRU_BLOCKS)))
    for k in range(LRU_BLOCKS):
        hc_s[k] = hs[k]


def _lru_fwd_kernel(x_ref, prev_ref, next_ref, cw_ref, cb_ref, wg_ref, ba_ref, bx_ref, lam_ref,
                    hf_ref, xc_ref, bm_s, xt_s, u_s, hc_s):
    nb = x_ref.shape[0]
    tc, pitch, halo = LRU_TC, LRU_PITCH, LRU_HALO
    ext = tc + 2 * halo
    chunk = pl.program_id(0)
    n = pl.num_programs(0)

    @pl.when(chunk == 0)
    def _():
        hc_s[...] = jnp.zeros_like(hc_s)

    for k in range(LRU_BLOCKS):
        cols = slice(k * LRU_BLOCK_W, (k + 1) * LRU_BLOCK_W)
        for b in range(nb):
            base = b * pitch
            bm_s[k, base:base + halo, :] = jnp.where(chunk == 0, 0.0, prev_ref[b, :, cols])
            bm_s[k, base + halo:base + halo + tc, :] = x_ref[b, :, cols]
            bm_s[k, base + halo + tc:base + ext, :] = jnp.where(chunk == n - 1, 0.0, next_ref[b, :, cols])

    def to_time_major(tau, carry):
        dst = pl.ds(pl.multiple_of(tau * nb, nb), nb)
        for k in range(LRU_BLOCKS):
            xt_s[k, dst, :] = bm_s[k, pl.ds(tau, nb, stride=pitch), :]
        return carry

    lax.fori_loop(0, ext, to_time_major, 0, unroll=8)

    for k in range(LRU_BLOCKS):
        cols = slice(k * LRU_BLOCK_W, (k + 1) * LRU_BLOCK_W)
        xb = cb_ref[:, cols]
        for j in range(cw_ref.shape[0]):
            r0 = (halo - 2 + j) * nb
            xb = xb + xt_s[k, r0:r0 + tc * nb, :] * cw_ref[j:j + 1, cols]
        xc_ref[0, k] = xb
        a, u = _lru_gates(xb, k, wg_ref, ba_ref, bx_ref, lam_ref)
        bm_s[k, 0:tc * nb, :] = a
        u_s[k] = u

    _lru_scan(bm_s, u_s, hc_s, nb, reverse=False)

    for k in range(LRU_BLOCKS):
        cols = slice(k * LRU_BLOCK_W, (k + 1) * LRU_BLOCK_W)
        for b in range(nb):
            hf_ref[b, :, cols] = u_s[k, pl.ds(b, tc, stride=nb), :]


def _lru_bwd_kernel(xc_ref, wg_ref, ba_ref, bx_ref, lam_ref, hf_ref, lg_ref, o_ref, a_s, u_s, hc_s):
    nb = hf_ref.shape[0]

    @pl.when(pl.program_id(0) == 0)
    def _():
        hc_s[...] = jnp.zeros_like(hc_s)

    for k in range(LRU_BLOCKS):
        a, u = _lru_gates(xc_ref[0, k], k, wg_ref, ba_ref, bx_ref, lam_ref)
        a_s[k] = a
        u_s[k] = u

    _lru_scan(a_s, u_s, hc_s, nb, reverse=True)

    for k in range(LRU_BLOCKS):
        cols = slice(k * LRU_BLOCK_W, (k + 1) * LRU_BLOCK_W)
        for b in range(nb):
            hk = u_s[k, pl.ds(b, LRU_TC, stride=nb), :]
            o_ref[b, :, cols] = ((hf_ref[b, :, cols] + hk) * _gelu_tanh(lg_ref[b, :, cols])).astype(BF16)


def _lru_call(lx, lg, conv_w, conv_b, w_gate, b_a, b_x, lam):
    b, s, w = lx.shape
    tc, halo = LRU_TC, LRU_HALO
    n = s // tc
    hb_per_chunk = tc // halo
    last_hb = s // halo - 1
    const2 = lambda i: (0, 0)
    const3 = lambda i: (0, 0, 0)
    slab_rows = b * tc
    vec = lambda a, d: a[d].reshape(1, w)
    gate_specs = [pl.BlockSpec(w_gate.shape[1:], const3)] + [pl.BlockSpec((1, w), const2)] * 3
    slab = lambda rows: pltpu.VMEM((LRU_BLOCKS, rows, LRU_BLOCK_W), F32)

    hf, xc = pl.pallas_call(
        _lru_fwd_kernel,
        grid=(n,),
        in_specs=[
            pl.BlockSpec((b, tc, w), lambda i: (0, i, 0)),
            pl.BlockSpec((b, halo, w), lambda i: (0, jnp.maximum(i * hb_per_chunk - 1, 0), 0)),
            pl.BlockSpec((b, halo, w), lambda i: (0, jnp.minimum((i + 1) * hb_per_chunk, last_hb), 0)),
            pl.BlockSpec(conv_w.shape, const2),
            pl.BlockSpec((1, w), const2),
        ] + gate_specs,
        out_specs=[
            pl.BlockSpec((b, tc, w), lambda i: (0, i, 0)),
            pl.BlockSpec((1, LRU_BLOCKS, slab_rows, LRU_BLOCK_W), lambda i: (i, 0, 0, 0)),
        ],
        out_shape=[
            jax.ShapeDtypeStruct((b, s, w), F32),
            jax.ShapeDtypeStruct((n, LRU_BLOCKS, slab_rows, LRU_BLOCK_W), F32),
        ],
        scratch_shapes=[slab(b * LRU_PITCH), slab(b * (tc + 2 * halo)), slab(slab_rows), slab(b)],
        compiler_params=_cparams(("arbitrary",)),
        name="lru_fwd",
    )(lx, lx, lx, conv_w, conv_b.reshape(1, w), w_gate[0], vec(b_a, 0), vec(b_x, 0), vec(lam, 0))

    rev_chunk = pl.BlockSpec((b, tc, w), lambda i: (0, n - 1 - i, 0))
    return pl.pallas_call(
        _lru_bwd_kernel,
        grid=(n,),
        in_specs=[pl.BlockSpec((1, LRU_BLOCKS, slab_rows, LRU_BLOCK_W), lambda i: (n - 1 - i, 0, 0, 0))]
        + gate_specs + [rev_chunk, rev_chunk],
        out_specs=rev_chunk,
        out_shape=jax.ShapeDtypeStruct((b, s, w), BF16),
        scratch_shapes=[slab(slab_rows), slab(slab_rows), slab(b)],
        compiler_params=_cparams(("arbitrary",)),
        name="lru_bwd",
    )(xc, w_gate[1], vec(b_a, 1), vec(b_x, 1), vec(lam, 1), hf, lg)


def _merge_kernel(attn_ref, lru_ref, gates_ref, x_ref, mod_ref, wa_ref, wl_ref, wo_ref, o_ref, acat_s):
    for hd in range(N_Q_HEADS):
        acat_s[:, hd * HEAD_DIM:(hd + 1) * HEAD_DIM] = attn_ref[0, hd]
    a = jnp.dot(acat_s[...], wa_ref[...], preferred_element_type=F32)
    r = jnp.dot(lru_ref[0], wl_ref[...], preferred_element_type=F32)
    g = _sigmoid(gates_ref[0])
    merged = g[:, :D_MODEL] * a + g[:, D_MODEL:] * r
    y = jnp.dot(merged.astype(BF16), wo_ref[...], preferred_element_type=F32)
    o_ref[0] = x_ref[0] + mod_ref[0, 5:6, :] * y


def _merge_call(attn, lru, gates, x, mod, w_attn_o, w_lru_o, w_out, l):
    b, s, d = x.shape
    tm = MERGE_TM
    tok = lambda bi, i: (bi, i, 0)
    return pl.pallas_call(
        _merge_kernel,
        grid=(b, s // tm),
        in_specs=[
            pl.BlockSpec((1, N_Q_HEADS, tm, HEAD_DIM), lambda bi, i: (bi, 0, i, 0)),
            pl.BlockSpec((1, tm, LRU_WIDTH), tok),
            pl.BlockSpec((1, tm, 2 * d), tok),
            pl.BlockSpec((1, tm, d), tok),
            pl.BlockSpec((1, N_MOD, d), lambda bi, i: (bi, 0, 0)),
            _layer_weight_spec(w_attn_o, l),
            _layer_weight_spec(w_lru_o, l),
            _layer_weight_spec(w_out, l),
        ],
        out_specs=pl.BlockSpec((1, tm, d), tok),
        out_shape=jax.ShapeDtypeStruct((b, s, d), F32),
        scratch_shapes=[pltpu.VMEM((tm, ATTN_WIDTH), BF16)],
        compiler_params=_cparams(("arbitrary", "arbitrary")),
        name="merge_out",
    )(attn, lru, gates, x, mod, w_attn_o, w_lru_o, w_out)


def _rope_tables(s):
    rows = s // GRID_W
    row_ids = jnp.broadcast_to(jnp.arange(rows, dtype=F32)[:, None], (rows, GRID_W)).reshape(s)
    col_ids = jnp.broadcast_to(jnp.arange(GRID_W, dtype=F32)[None, :], (rows, GRID_W)).reshape(s)
    axis_dim = HEAD_DIM // 2
    inv_freq = ROPE_THETA ** (-jnp.arange(0, axis_dim, 2, dtype=F32) / axis_dim)
    ang = jnp.concatenate([row_ids[:, None] * inv_freq, col_ids[:, None] * inv_freq], axis=-1)
    cos, sin = jnp.cos(ang), jnp.sin(ang)
    return jnp.concatenate([cos, cos], axis=-1), jnp.concatenate([-sin, sin], axis=-1)


def _deinterleave_heads(w):
    lead = w.shape[:-1]
    n_heads = w.shape[-1] // HEAD_DIM
    w = w.reshape(*lead, n_heads, HEAD_DIM // 2, 2)
    return jnp.swapaxes(w, -1, -2).reshape(*lead, n_heads * HEAD_DIM)


def kernel(x, c, ada_w, ada_b, norm_g, ffn1_up, ffn1_down, w_in, q_norm_g, k_norm_g, conv_w, conv_b,
           lru_wa, lru_ba, lru_wx, lru_bx, lru_lambda, w_attn_o, w_lru_o, w_out, ffn2_up, ffn2_down, final_g):
    b, s, d = x.shape
    depth = ada_w.shape[0]
    mod = _ada_call(c, ada_w, ada_b).reshape(depth, b, N_MOD, d)
    cos, sin = _rope_tables(s)

    ffn1_up, ffn2_up = ffn1_up.astype(BF16), ffn2_up.astype(BF16)
    ffn1_down, ffn2_down = ffn1_down.astype(BF16), ffn2_down.astype(BF16)
    w_qk = _deinterleave_heads(w_in[..., :OFF_V]).astype(BF16)
    w_rest = w_in[..., OFF_V:].astype(BF16)
    q_g, k_g = _deinterleave_heads(q_norm_g), _deinterleave_heads(k_norm_g)
    w_attn_o, w_lru_o, w_out = w_attn_o.astype(BF16), w_lru_o.astype(BF16), w_out.astype(BF16)
    w_gate = jnp.concatenate([lru_wa, lru_wx], axis=-1).astype(BF16)

    for l in range(depth):
        x = _ffn_call(x, mod[l], norm_g[l, 0], ffn1_up, ffn1_down, l, 0)
        q, k, v, lx, lg, gates = _proj_call(x, mod[l], norm_g[l, 1], w_qk, w_rest, l, q_g[l], k_g[l], cos, sin)
        attn = _attn_call(q, k, v)
        lru = _lru_call(lx, lg, conv_w[l], conv_b[l], w_gate[l], lru_ba[l], lru_bx[l], lru_lambda[l])
        x = _merge_call(attn, lru, gates, x, mod[l], w_attn_o, w_lru_o, w_out, l)
        x = _ffn_call(x, mod[l], norm_g[l, 2], ffn2_up, ffn2_down, l, 6,
                      final_g=final_g if l == depth - 1 else None)
    return x
```

```python
import functools
import math

import jax
import jax.numpy as jnp
from jax import lax
from jax.experimental import pallas as pl
from jax.experimental.pallas import tpu as pltpu

F32 = jnp.float32
BF16 = jnp.bfloat16

SUBLANES = 8
VMEM_LIMIT_BYTES = 56 * 1024 * 1024

D_MODEL = 1024
HEAD_DIM = 128
N_Q_HEADS = 8
N_KV_HEADS = 2
GROUP = N_Q_HEADS // N_KV_HEADS
ATTN_WIDTH = N_Q_HEADS * HEAD_DIM
KV_WIDTH = N_KV_HEADS * HEAD_DIM
ROPE_THETA = 10000.0
GRID_W = 64
LRU_WIDTH = D_MODEL
LRU_BLOCKS = 8
LRU_BLOCK_W = LRU_WIDTH // LRU_BLOCKS
LRU_C = 8.0
D_FF = 2816
FFN_RES = 0.5
N_MOD = 9
EPS = 1e-6
Q_PRESCALE = (HEAD_DIM ** -0.5) * math.log2(math.e)

OFF_Q = 0
OFF_K = ATTN_WIDTH
OFF_V = OFF_K + KV_WIDTH
OFF_LX = OFF_V + KV_WIDTH
OFF_LG = OFF_LX + LRU_WIDTH
OFF_GATES = OFF_LG + LRU_WIDTH
IN_COLS = OFF_GATES + 2 * D_MODEL

ADA_TN = 2304
FFN_TM = 1024
FFN_TF = 1024
FFN_CHUNKS = tuple((lo, min(lo + FFN_TF, D_FF)) for lo in range(0, D_FF, FFN_TF))
PROJ_TM = 512
ATT_TQ = 4096
ATT_M = 512
ATT_KC = 512
ATT_UNROLL = 8
LRU_TC = 128
LRU_HALO = SUBLANES
LRU_PITCH = LRU_TC + 2 * LRU_HALO + SUBLANES
MERGE_TM = 512


def _cparams(semantics):
    return pltpu.CompilerParams(dimension_semantics=semantics, vmem_limit_bytes=VMEM_LIMIT_BYTES)


def _sigmoid(x):
    return 1.0 / (1.0 + jnp.exp2(x * (-math.log2(math.e))))


def _rms_modulate(x, g, shift, scale):
    y = x * lax.rsqrt(jnp.mean(x * x, axis=-1, keepdims=True) + EPS) * g
    return y * (1.0 + scale) + shift


def _ada_kernel(c_ref, w_ref, b_ref, o_ref):
    c = c_ref[...]
    c_act = (c * _sigmoid(c)).astype(BF16)
    o_ref[0] = jnp.dot(c_act, w_ref[0].astype(BF16), preferred_element_type=F32) + b_ref[0]


def _ada_call(c, ada_w, ada_b):
    depth, d, n = ada_w.shape
    b = c.shape[0]
    return pl.pallas_call(
        _ada_kernel,
        grid=(depth, n // ADA_TN),
        in_specs=[
            pl.BlockSpec((b, d), lambda l, j: (0, 0)),
            pl.BlockSpec((1, d, ADA_TN), lambda l, j: (l, 0, j)),
            pl.BlockSpec((1, 1, ADA_TN), lambda l, j: (l, 0, j)),
        ],
        out_specs=pl.BlockSpec((1, b, ADA_TN), lambda l, j: (l, 0, j)),
        out_shape=jax.ShapeDtypeStruct((depth, b, n), F32),
        compiler_params=_cparams(("arbitrary", "arbitrary")),
        name="ada_mod",
    )(c, ada_w, ada_b.reshape(depth, 1, n))


def _ffn_kernel(*refs, mod_base, final_norm):
    if final_norm:
        x_ref, mod_ref, g_ref, wup_ref, wd_ref, fg_ref, o_ref, acc_s = refs
    else:
        x_ref, mod_ref, g_ref, wup_ref, wd_ref, o_ref, acc_s = refs
    shift = mod_ref[0, mod_base:mod_base + 1, :]
    scale = mod_ref[0, mod_base + 1:mod_base + 2, :]
    h = _rms_modulate(x_ref[0], g_ref[...], shift, scale).astype(BF16)
    for n, (lo, hi) in enumerate(FFN_CHUNKS):
        gate = jnp.dot(h, wup_ref[:, lo:hi], preferred_element_type=F32)
        up = jnp.dot(h, wup_ref[:, D_FF + lo:D_FF + hi], preferred_element_type=F32)
        act = (gate * _sigmoid(gate) * up).astype(BF16)
        part = jnp.dot(act, wd_ref[lo:hi, :], preferred_element_type=F32)
        if n == 0:
            acc_s[...] = part
        else:
            acc_s[...] += part
    res_gate = mod_ref[0, mod_base + 2:mod_base + 3, :]
    out = x_ref[0] + (FFN_RES * res_gate) * acc_s[...]
    if final_norm:
        out = out * lax.rsqrt(jnp.mean(out * out, axis=-1, keepdims=True) + EPS) * fg_ref[...]
    o_ref[0] = out


def _layer_weight_spec(w, l):
    return pl.BlockSpec((None,) + w.shape[1:], lambda *_: (l, 0, 0), pipeline_mode=pl.Buffered(1))


def _ffn_call(x, mod, g, w_up, w_down, l, mod_base, final_g=None):
    b, s, d = x.shape
    final_norm = final_g is not None
    const2 = lambda bi, i: (0, 0)
    in_specs = [
        pl.BlockSpec((1, FFN_TM, d), lambda bi, i: (bi, i, 0)),
        pl.BlockSpec((1, N_MOD, d), lambda bi, i: (bi, 0, 0)),
        pl.BlockSpec((1, d), const2),
        _layer_weight_spec(w_up, l),
        _layer_weight_spec(w_down, l),
    ]
    args = [x, mod, g.reshape(1, d), w_up, w_down]
    if final_norm:
        in_specs.append(pl.BlockSpec((1, d), const2))
        args.append(final_g.reshape(1, d))
    return pl.pallas_call(
        functools.partial(_ffn_kernel, mod_base=mod_base, final_norm=final_norm),
        grid=(b, s // FFN_TM),
        in_specs=in_specs,
        out_specs=pl.BlockSpec((1, FFN_TM, d), lambda bi, i: (bi, i, 0)),
        out_shape=jax.ShapeDtypeStruct((b, s, d), F32),
        scratch_shapes=[pltpu.VMEM((FFN_TM, d), F32)],
        compiler_params=_cparams(("arbitrary", "arbitrary")),
        name="ffn_final" if final_norm else "ffn",
    )(*args)


def _head_norm_rope(xh, g, cos, sin):
    y = xh * lax.rsqrt(jnp.mean(xh * xh, axis=-1, keepdims=True) + EPS) * g
    return y * cos + pltpu.roll(y, HEAD_DIM // 2, 1) * sin


def _proj_kernel(x_ref, mod_ref, g_ref, wqk_ref, wr_ref, qg_ref, kg_ref, cos_ref, sin_ref,
                 q_ref, k_ref, v_ref, lx_ref, lg_ref, gates_ref):
    h = _rms_modulate(x_ref[0], g_ref[...], mod_ref[0, 3:4, :], mod_ref[0, 4:5, :]).astype(BF16)
    cos = cos_ref[...]
    sin = sin_ref[...]

    q = jnp.dot(h, wqk_ref[:, OFF_Q:OFF_K], preferred_element_type=F32)
    for hd in range(N_Q_HEADS):
        qh = q[:, hd * HEAD_DIM:(hd + 1) * HEAD_DIM]
        q_ref[0, hd] = (_head_norm_rope(qh, qg_ref[...], cos, sin) * Q_PRESCALE).astype(BF16)

    k = jnp.dot(h, wqk_ref[:, OFF_K:OFF_V], preferred_element_type=F32)
    v = jnp.dot(h, wr_ref[:, 0:KV_WIDTH], preferred_element_type=F32)
    for hd in range(N_KV_HEADS):
        head = slice(hd * HEAD_DIM, (hd + 1) * HEAD_DIM)
        k_ref[0, hd] = _head_norm_rope(k[:, head], kg_ref[...], cos, sin).astype(BF16)
        v_ref[0, hd] = v[:, head].astype(BF16)

    lx_ref[0] = jnp.dot(h, wr_ref[:, OFF_LX - OFF_V:OFF_LG - OFF_V], preferred_element_type=F32)
    lg_ref[0] = jnp.dot(h, wr_ref[:, OFF_LG - OFF_V:OFF_GATES - OFF_V], preferred_element_type=F32)
    gates_ref[0] = jnp.dot(h, wr_ref[:, OFF_GATES - OFF_V:IN_COLS - OFF_V], preferred_element_type=F32)


def _proj_call(x, mod, g, w_qk, w_rest, l, q_g, k_g, cos, sin):
    b, s, d = x.shape
    tm = PROJ_TM
    const2 = lambda bi, i: (0, 0)
    return pl.pallas_call(
        _proj_kernel,
        grid=(b, s // tm),
        in_specs=[
            pl.BlockSpec((1, tm, d), lambda bi, i: (bi, i, 0)),
            pl.BlockSpec((1, N_MOD, d), lambda bi, i: (bi, 0, 0)),
            pl.BlockSpec((1, d), const2),
            _layer_weight_spec(w_qk, l),
            _layer_weight_spec(w_rest, l),
            pl.BlockSpec((1, HEAD_DIM), const2),
            pl.BlockSpec((1, HEAD_DIM), const2),
            pl.BlockSpec((tm, HEAD_DIM), lambda bi, i: (i, 0)),
            pl.BlockSpec((tm, HEAD_DIM), lambda bi, i: (i, 0)),
        ],
        out_specs=[
            pl.BlockSpec((1, N_Q_HEADS, tm, HEAD_DIM), lambda bi, i: (bi, 0, i, 0)),
            pl.BlockSpec((1, N_KV_HEADS, tm, HEAD_DIM), lambda bi, i: (bi, 0, i, 0)),
            pl.BlockSpec((1, N_KV_HEADS, tm, HEAD_DIM), lambda bi, i: (bi, 0, i, 0)),
            pl.BlockSpec((1, tm, LRU_WIDTH), lambda bi, i: (bi, i, 0)),
            pl.BlockSpec((1, tm, LRU_WIDTH), lambda bi, i: (bi, i, 0)),
            pl.BlockSpec((1, tm, 2 * d), lambda bi, i: (bi, i, 0)),
        ],
        out_shape=[
            jax.ShapeDtypeStruct((b, N_Q_HEADS, s, HEAD_DIM), BF16),
            jax.ShapeDtypeStruct((b, N_KV_HEADS, s, HEAD_DIM), BF16),
            jax.ShapeDtypeStruct((b, N_KV_HEADS, s, HEAD_DIM), BF16),
            jax.ShapeDtypeStruct((b, s, LRU_WIDTH), F32),
            jax.ShapeDtypeStruct((b, s, LRU_WIDTH), F32),
            jax.ShapeDtypeStruct((b, s, 2 * d), F32),
        ],
        compiler_params=_cparams(("arbitrary", "arbitrary")),
        name="in_proj",
    )(x, mod, g.reshape(1, d), w_qk, w_rest, q_g.reshape(1, HEAD_DIM), k_g.reshape(1, HEAD_DIM), cos, sin)


def _attn_kernel(q_ref, k_ref, v_ref, o_ref, s_scr, m_scr, vext_s):
    n_kc = k_ref.shape[2] // ATT_KC
    row_blocks = ATT_TQ // ATT_M
    n_items = GROUP * row_blocks

    vext_s[:, :HEAD_DIM] = v_ref[0, 0]
    vext_s[:, HEAD_DIM:] = jnp.ones((vext_s.shape[0], HEAD_DIM), BF16)

    def item_rows(item):
        g = item // row_blocks
        r0 = pl.multiple_of((item % row_blocks) * ATT_M, ATT_M)
        return g, pl.ds(r0, ATT_M)

    def scores(item, slot):
        g, rows = item_rows(item)
        q = q_ref[0, g, rows, :]
        m = None
        for c in range(n_kc):
            kc = k_ref[0, 0, c * ATT_KC:(c + 1) * ATT_KC, :]
            s = lax.dot_general(q, kc, (((1,), (1,)), ((), ())), preferred_element_type=F32)
            s_scr[slot, :, c * ATT_KC:(c + 1) * ATT_KC] = s
            mc = jnp.max(s, axis=-1, keepdims=True)
            m = mc if m is None else jnp.maximum(m, mc)
        m_scr[slot] = m

    def finish(item, slot):
        g, rows = item_rows(item)
        m = m_scr[slot]
        acc = jnp.zeros((ATT_M, 2 * HEAD_DIM), F32)
        for c in range(n_kc):
            p = jnp.exp2(s_scr[slot, :, c * ATT_KC:(c + 1) * ATT_KC] - m)
            vc = vext_s[c * ATT_KC:(c + 1) * ATT_KC, :]
            acc = acc + jnp.dot(p.astype(BF16), vc, preferred_element_type=F32)
        o_ref[0, g, rows, :] = (acc[:, :HEAD_DIM] / acc[:, HEAD_DIM:]).astype(BF16)

    scores(jnp.int32(0), 0)

    def group_body(j, carry):
        first = ATT_UNROLL * j
        for u in range(ATT_UNROLL):
            scores(jnp.minimum(first + u + 1, n_items - 1), (u + 1) % 2)
            finish(first + u, u % 2)
        return carry

    lax.fori_loop(0, n_items // ATT_UNROLL, group_body, 0)


def _attn_call(q, k, v):
    b, _, s, _ = q.shape
    return pl.pallas_call(
        _attn_kernel,
        grid=(b, N_KV_HEADS, s // ATT_TQ),
        in_specs=[
            pl.BlockSpec((1, GROUP, ATT_TQ, HEAD_DIM), lambda bi, kh, i: (bi, kh, i, 0)),
            pl.BlockSpec((1, 1, s, HEAD_DIM), lambda bi, kh, i: (bi, kh, 0, 0)),
            pl.BlockSpec((1, 1, s, HEAD_DIM), lambda bi, kh, i: (bi, kh, 0, 0)),
        ],
        out_specs=pl.BlockSpec((1, GROUP, ATT_TQ, HEAD_DIM), lambda bi, kh, i: (bi, kh, i, 0)),
        out_shape=jax.ShapeDtypeStruct((b, N_Q_HEADS, s, HEAD_DIM), BF16),
        scratch_shapes=[pltpu.VMEM((2, ATT_M, s), F32), pltpu.VMEM((2, ATT_M, 1), F32),
                        pltpu.VMEM((s, 2 * HEAD_DIM), BF16)],
        compiler_params=_cparams(("arbitrary", "arbitrary", "arbitrary")),
        name="attention",
    )(q, k, v)


def _gelu_tanh(x):
    c = math.sqrt(2.0 / math.pi)
    half_x = 0.5 * x
    return half_x + half_x * jnp.tanh(x * (c + (0.044715 * c) * (x * x)))


def _lru_gates(xb, k, wg_ref, ba_ref, bx_ref, lam_ref):
    cols = slice(k * LRU_BLOCK_W, (k + 1) * LRU_BLOCK_W)
    pre = jnp.dot(xb.astype(BF16), wg_ref[k], preferred_element_type=F32)
    r = _sigmoid(pre[:, :LRU_BLOCK_W] + ba_ref[:, cols])
    ig = _sigmoid(pre[:, LRU_BLOCK_W:] + bx_ref[:, cols])
    z = -lam_ref[:, cols]
    softplus = jnp.maximum(z, 0.0) + jnp.log1p(jnp.exp(-jnp.abs(z)))
    a = jnp.exp2(r * ((-LRU_C * math.log2(math.e)) * softplus))
    one_m_a2 = 1.0 - a * a
    root = jnp.where(one_m_a2 > 0.0, one_m_a2 * lax.rsqrt(one_m_a2), 0.0)
    return a, root * (ig * xb)


def _lru_scan_block(a_s, u_s, hc_s, k, nb, reverse):
    h = hc_s[k]
    for t in (reversed(range(LRU_TC)) if reverse else range(LRU_TC)):
        rows = slice(t * nb, (t + 1) * nb)
        h = a_s[k, rows, :] * h + u_s[k, rows, :]
        u_s[k, rows, :] = h
    hc_s[k] = h


def _lru_fwd_kernel(x_ref, prev_ref, next_ref, cw_ref, cb_ref, wg_ref, ba_ref, bx_ref, lam_ref,
                    hf_ref, xc_ref, bm_s, xt_s, u_s, hc_s):
    nb = x_ref.shape[0]
    tc, pitch, halo = LRU_TC, LRU_PITCH, LRU_HALO
    ext = tc + 2 * halo
    n_taps = cw_ref.shape[0]
    chunk = pl.program_id(0)
    n = pl.num_programs(0)

    @pl.when(chunk == 0)
    def _():
        hc_s[...] = jnp.zeros_like(hc_s)

    for k in range(LRU_BLOCKS):
        cols = slice(k * LRU_BLOCK_W, (k + 1) * LRU_BLOCK_W)
        for b in range(nb):
            base = b * pitch
            bm_s[k, base:base + halo, :] = jnp.where(chunk == 0, 0.0, prev_ref[b, :, cols])
            bm_s[k, base + halo:base + halo + tc, :] = x_ref[b, :, cols]
            bm_s[k, base + halo + tc:base + ext, :] = jnp.where(chunk == n - 1, 0.0, next_ref[b, :, cols])
        for tau in range(halo - 2, halo + tc + n_taps - 2):
            xt_s[k, tau * nb:(tau + 1) * nb, :] = bm_s[k, pl.ds(tau, nb, stride=pitch), :]
        xb = cb_ref[:, cols]
        for j in range(n_taps):
            r0 = (halo - 2 + j) * nb
            xb = xb + xt_s[k, r0:r0 + tc * nb, :] * cw_ref[j:j + 1, cols]
        xc_ref[0, k] = xb
        a, u = _lru_gates(xb, k, wg_ref, ba_ref, bx_ref, lam_ref)
        bm_s[k, 0:tc * nb, :] = a
        u_s[k] = u
        _lru_scan_block(bm_s, u_s, hc_s, k, nb, reverse=False)
        for b in range(nb):
            hf_ref[b, :, cols] = u_s[k, pl.ds(b, tc, stride=nb), :]


def _lru_bwd_kernel(xc_ref, wg_ref, ba_ref, bx_ref, lam_ref, hf_ref, lg_ref, o_ref, a_s, u_s, hc_s):
    nb = hf_ref.shape[0]

    @pl.when(pl.program_id(0) == 0)
    def _():
        hc_s[...] = jnp.zeros_like(hc_s)

    for k in range(LRU_BLOCKS):
        cols = slice(k * LRU_BLOCK_W, (k + 1) * LRU_BLOCK_W)
        a, u = _lru_gates(xc_ref[0, k], k, wg_ref, ba_ref, bx_ref, lam_ref)
        a_s[k] = a
        u_s[k] = u
        _lru_scan_block(a_s, u_s, hc_s, k, nb, reverse=True)
        for b in range(nb):
            hk = u_s[k, pl.ds(b, LRU_TC, stride=nb), :]
            o_ref[b, :, cols] = ((hf_ref[b, :, cols] + hk) * _gelu_tanh(lg_ref[b, :, cols])).astype(BF16)


def _lru_call(lx, lg, conv_w, conv_b, w_gate, b_a, b_x, lam):
    b, s, w = lx.shape
    tc, halo = LRU_TC, LRU_HALO
    n = s // tc
    hb_per_chunk = tc // halo
    last_hb = s // halo - 1
    const2 = lambda i: (0, 0)
    const3 = lambda i: (0, 0, 0)
    slab_rows = b * tc
    vec = lambda a, d: a[d].reshape(1, w)
    gate_specs = [pl.BlockSpec(w_gate.shape[1:], const3)] + [pl.BlockSpec((1, w), const2)] * 3
    slab = lambda rows: pltpu.VMEM((LRU_BLOCKS, rows, LRU_BLOCK_W), F32)

    hf, xc = pl.pallas_call(
        _lru_fwd_kernel,
        grid=(n,),
        in_specs=[
            pl.BlockSpec((b, tc, w), lambda i: (0, i, 0)),
            pl.BlockSpec((b, halo, w), lambda i: (0, jnp.maximum(i * hb_per_chunk - 1, 0), 0)),
            pl.BlockSpec((b, halo, w), lambda i: (0, jnp.minimum((i + 1) * hb_per_chunk, last_hb), 0)),
            pl.BlockSpec(conv_w.shape, const2),
            pl.BlockSpec((1, w), const2),
        ] + gate_specs,
        out_specs=[
            pl.BlockSpec((b, tc, w), lambda i: (0, i, 0)),
            pl.BlockSpec((1, LRU_BLOCKS, slab_rows, LRU_BLOCK_W), lambda i: (i, 0, 0, 0)),
        ],
        out_shape=[
            jax.ShapeDtypeStruct((b, s, w), F32),
            jax.ShapeDtypeStruct((n, LRU_BLOCKS, slab_rows, LRU_BLOCK_W), F32),
        ],
        scratch_shapes=[slab(b * LRU_PITCH), slab(b * (tc + 2 * halo)), slab(slab_rows), slab(b)],
        compiler_params=_cparams(("arbitrary",)),
        name="lru_fwd",
    )(lx, lx, lx, conv_w, conv_b.reshape(1, w), w_gate[0], vec(b_a, 0), vec(b_x, 0), vec(lam, 0))

    rev_chunk = pl.BlockSpec((b, tc, w), lambda i: (0, n - 1 - i, 0))
    return pl.pallas_call(
        _lru_bwd_kernel,
        grid=(n,),
        in_specs=[pl.BlockSpec((1, LRU_BLOCKS, slab_rows, LRU_BLOCK_W), lambda i: (n - 1 - i, 0, 0, 0))]
        + gate_specs + [rev_chunk, rev_chunk],
        out_specs=rev_chunk,
        out_shape=jax.ShapeDtypeStruct((b, s, w), BF16),
        scratch_shapes=[slab(slab_rows), slab(slab_rows), slab(b)],
        compiler_params=_cparams(("arbitrary",)),
        name="lru_bwd",
    )(xc, w_gate[1], vec(b_a, 1), vec(b_x, 1), vec(lam, 1), hf, lg)


def _merge_kernel(attn_ref, lru_ref, gates_ref, x_ref, mod_ref, wa_ref, wl_ref, wo_ref, o_ref, acat_s):
    for hd in range(N_Q_HEADS):
        acat_s[:, hd * HEAD_DIM:(hd + 1) * HEAD_DIM] = attn_ref[0, hd]
    a = jnp.dot(acat_s[...], wa_ref[...], preferred_element_type=F32)
    r = jnp.dot(lru_ref[0], wl_ref[...], preferred_element_type=F32)
    g = _sigmoid(gates_ref[0])
    merged = g[:, :D_MODEL] * a + g[:, D_MODEL:] * r
    y = jnp.dot(merged.astype(BF16), wo_ref[...], preferred_element_type=F32)
    o_ref[0] = x_ref[0] + mod_ref[0, 5:6, :] * y


def _merge_call(attn, lru, gates, x, mod, w_attn_o, w_lru_o, w_out, l):
    b, s, d = x.shape
    tm = MERGE_TM
    tok = lambda bi, i: (bi, i, 0)
    return pl.pallas_call(
        _merge_kernel,
        grid=(b, s // tm),
        in_specs=[
            pl.BlockSpec((1, N_Q_HEADS, tm, HEAD_DIM), lambda bi, i: (bi, 0, i, 0)),
            pl.BlockSpec((1, tm, LRU_WIDTH), tok),
            pl.BlockSpec((1, tm, 2 * d), tok),
            pl.BlockSpec((1, tm, d), tok),
            pl.BlockSpec((1, N_MOD, d), lambda bi, i: (bi, 0, 0)),
            _layer_weight_spec(w_attn_o, l),
            _layer_weight_spec(w_lru_o, l),
            _layer_weight_spec(w_out, l),
        ],
        out_specs=pl.BlockSpec((1, tm, d), tok),
        out_shape=jax.ShapeDtypeStruct((b, s, d), F32),
        scratch_shapes=[pltpu.VMEM((tm, ATTN_WIDTH), BF16)],
        compiler_params=_cparams(("arbitrary", "arbitrary")),
        name="merge_out",
    )(attn, lru, gates, x, mod, w_attn_o, w_lru_o, w_out)


def _rope_tables(s):
    rows = s // GRID_W
    row_ids = jnp.broadcast_to(jnp.arange(rows, dtype=F32)[:, None], (rows, GRID_W)).reshape(s)
    col_ids = jnp.broadcast_to(jnp.arange(GRID_W, dtype=F32)[None, :], (rows, GRID_W)).reshape(s)
    axis_dim = HEAD_DIM // 2
    inv_freq = ROPE_THETA ** (-jnp.arange(0, axis_dim, 2, dtype=F32) / axis_dim)
    ang = jnp.concatenate([row_ids[:, None] * inv_freq, col_ids[:, None] * inv_freq], axis=-1)
    cos, sin = jnp.cos(ang), jnp.sin(ang)
    return jnp.concatenate([cos, cos], axis=-1), jnp.concatenate([-sin, sin], axis=-1)


def _deinterleave_heads(w):
    lead = w.shape[:-1]
    n_heads = w.shape[-1] // HEAD_DIM
    w = w.reshape(*lead, n_heads, HEAD_DIM // 2, 2)
    return jnp.swapaxes(w, -1, -2).reshape(*lead, n_heads * HEAD_DIM)


def kernel(x, c, ada_w, ada_b, norm_g, ffn1_up, ffn1_down, w_in, q_norm_g, k_norm_g, conv_w, conv_b,
           lru_wa, lru_ba, lru_wx, lru_bx, lru_lambda, w_attn_o, w_lru_o, w_out, ffn2_up, ffn2_down, final_g):
    b, s, d = x.shape
    depth = ada_w.shape[0]
    mod = _ada_call(c, ada_w, ada_b).reshape(depth, b, N_MOD, d)
    cos, sin = _rope_tables(s)

    ffn1_up, ffn2_up = ffn1_up.astype(BF16), ffn2_up.astype(BF16)
    ffn1_down, ffn2_down = ffn1_down.astype(BF16), ffn2_down.astype(BF16)
    w_qk = _deinterleave_heads(w_in[..., :OFF_V]).astype(BF16)
    w_rest = w_in[..., OFF_V:].astype(BF16)
    q_g, k_g = _deinterleave_heads(q_norm_g), _deinterleave_heads(k_norm_g)
    w_attn_o, w_lru_o, w_out = w_attn_o.astype(BF16), w_lru_o.astype(BF16), w_out.astype(BF16)
    w_gate = jnp.concatenate([lru_wa, lru_wx], axis=-1).astype(BF16)

    for l in range(depth):
        x = _ffn_call(x, mod[l], norm_g[l, 0], ffn1_up, ffn1_down, l, 0)
        q, k, v, lx, lg, gates = _proj_call(x, mod[l], norm_g[l, 1], w_qk, w_rest, l, q_g[l], k_g[l], cos, sin)
        attn = _attn_call(q, k, v)
        lru = _lru_call(lx, lg, conv_w[l], conv_b[l], w_gate[l], lru_ba[l], lru_bx[l], lru_lambda[l])
        x = _merge_call(attn, lru, gates, x, mod[l], w_attn_o, w_lru_o, w_out, l)
        x = _ffn_call(x, mod[l], norm_g[l, 2], ffn2_up, ffn2_down, l, 6,
                      final_g=final_g if l == depth - 1 else None)
    return x
```

```python
import functools
import math

import jax
import jax.numpy as jnp
from jax import lax
from jax.experimental import pallas as pl
from jax.experimental.pallas import tpu as pltpu

F32 = jnp.float32
BF16 = jnp.bfloat16

SUBLANES = 8
VMEM_LIMIT_BYTES = 56 * 1024 * 1024

D_MODEL = 1024
HEAD_DIM = 128
N_Q_HEADS = 8
N_KV_HEADS = 2
GROUP = N_Q_HEADS // N_KV_HEADS
ATTN_WIDTH = N_Q_HEADS * HEAD_DIM
KV_WIDTH = N_KV_HEADS * HEAD_DIM
ROPE_THETA = 10000.0
GRID_W = 64
LRU_WIDTH = D_MODEL
LRU_BLOCKS = 8
LRU_BLOCK_W = LRU_WIDTH // LRU_BLOCKS
LRU_C = 8.0
D_FF = 2816
FFN_RES = 0.5
N_MOD = 9
EPS = 1e-6
Q_PRESCALE = (HEAD_DIM ** -0.5) * math.log2(math.e)

OFF_Q = 0
OFF_K = ATTN_WIDTH
OFF_V = OFF_K + KV_WIDTH
OFF_LX = OFF_V + KV_WIDTH
OFF_LG = OFF_LX + LRU_WIDTH
OFF_GATES = OFF_LG + LRU_WIDTH
IN_COLS = OFF_GATES + 2 * D_MODEL

ADA_TN = 2304
FFN_TM = 1024
FFN_TF = 1024
FFN_CHUNKS = tuple((lo, min(lo + FFN_TF, D_FF)) for lo in range(0, D_FF, FFN_TF))
PROJ_TM = 512
ATT_TQ = 4096
ATT_M = 512
ATT_KC = 512
ATT_UNROLL = 16
LRU_TC = 128
LRU_HALO = SUBLANES
LRU_PITCH = LRU_TC + 2 * LRU_HALO + SUBLANES
MERGE_TM = 512


def _cparams(semantics):
    return pltpu.CompilerParams(dimension_semantics=semantics, vmem_limit_bytes=VMEM_LIMIT_BYTES)


def _sigmoid(x):
    return 1.0 / (1.0 + jnp.exp2(x * (-math.log2(math.e))))


def _rms_modulate(x, g, shift, scale):
    y = x * lax.rsqrt(jnp.mean(x * x, axis=-1, keepdims=True) + EPS) * g
    return y * (1.0 + scale) + shift


def _ada_kernel(c_ref, w_ref, b_ref, o_ref):
    c = c_ref[...]
    c_act = (c * _sigmoid(c)).astype(BF16)
    o_ref[0] = jnp.dot(c_act, w_ref[0].astype(BF16), preferred_element_type=F32) + b_ref[0]


def _ada_call(c, ada_w, ada_b):
    depth, d, n = ada_w.shape
    b = c.shape[0]
    return pl.pallas_call(
        _ada_kernel,
        grid=(depth, n // ADA_TN),
        in_specs=[
            pl.BlockSpec((b, d), lambda l, j: (0, 0)),
            pl.BlockSpec((1, d, ADA_TN), lambda l, j: (l, 0, j)),
            pl.BlockSpec((1, 1, ADA_TN), lambda l, j: (l, 0, j)),
        ],
        out_specs=pl.BlockSpec((1, b, ADA_TN), lambda l, j: (l, 0, j)),
        out_shape=jax.ShapeDtypeStruct((depth, b, n), F32),
        compiler_params=_cparams(("arbitrary", "arbitrary")),
        name="ada_mod",
    )(c, ada_w, ada_b.reshape(depth, 1, n))


def _ffn_kernel(*refs, mod_base, final_norm):
    if final_norm:
        x_ref, mod_ref, g_ref, wup_ref, wd_ref, fg_ref, o_ref, acc_s = refs
    else:
        x_ref, mod_ref, g_ref, wup_ref, wd_ref, o_ref, acc_s = refs
    shift = mod_ref[0, mod_base:mod_base + 1, :]
    scale = mod_ref[0, mod_base + 1:mod_base + 2, :]
    h = _rms_modulate(x_ref[0], g_ref[...], shift, scale).astype(BF16)
    for n, (lo, hi) in enumerate(FFN_CHUNKS):
        gate = jnp.dot(h, wup_ref[:, lo:hi], preferred_element_type=F32)
        up = jnp.dot(h, wup_ref[:, D_FF + lo:D_FF + hi], preferred_element_type=F32)
        act = (gate * _sigmoid(gate) * up).astype(BF16)
        part = jnp.dot(act, wd_ref[lo:hi, :], preferred_element_type=F32)
        if n == 0:
            acc_s[...] = part
        else:
            acc_s[...] += part
    res_gate = mod_ref[0, mod_base + 2:mod_base + 3, :]
    out = x_ref[0] + (FFN_RES * res_gate) * acc_s[...]
    if final_norm:
        out = out * lax.rsqrt(jnp.mean(out * out, axis=-1, keepdims=True) + EPS) * fg_ref[...]
    o_ref[0] = out


def _layer_weight_spec(w, l):
    return pl.BlockSpec((None,) + w.shape[1:], lambda *_: (l, 0, 0), pipeline_mode=pl.Buffered(1))


def _ffn_call(x, mod, g, w_up, w_down, l, mod_base, final_g=None):
    b, s, d = x.shape
    final_norm = final_g is not None
    const2 = lambda bi, i: (0, 0)
    in_specs = [
        pl.BlockSpec((1, FFN_TM, d), lambda bi, i: (bi, i, 0)),
        pl.BlockSpec((1, N_MOD, d), lambda bi, i: (bi, 0, 0)),
        pl.BlockSpec((1, d), const2),
        _layer_weight_spec(w_up, l),
        _layer_weight_spec(w_down, l),
    ]
    args = [x, mod, g.reshape(1, d), w_up, w_down]
    if final_norm:
        in_specs.append(pl.BlockSpec((1, d), const2))
        args.append(final_g.reshape(1, d))
    return pl.pallas_call(
        functools.partial(_ffn_kernel, mod_base=mod_base, final_norm=final_norm),
        grid=(b, s // FFN_TM),
        in_specs=in_specs,
        out_specs=pl.BlockSpec((1, FFN_TM, d), lambda bi, i: (bi, i, 0)),
        out_shape=jax.ShapeDtypeStruct((b, s, d), F32),
        scratch_shapes=[pltpu.VMEM((FFN_TM, d), F32)],
        compiler_params=_cparams(("arbitrary", "arbitrary")),
        name="ffn_final" if final_norm else "ffn",
    )(*args)


def _head_norm_rope(xh, g, cos, sin):
    y = xh * lax.rsqrt(jnp.mean(xh * xh, axis=-1, keepdims=True) + EPS) * g
    return y * cos + pltpu.roll(y, HEAD_DIM // 2, 1) * sin


def _proj_kernel(x_ref, mod_ref, g_ref, wqk_ref, wr_ref, qg_ref, kg_ref, cos_ref, sin_ref,
                 q_ref, k_ref, v_ref, lx_ref, lg_ref, gates_ref):
    h = _rms_modulate(x_ref[0], g_ref[...], mod_ref[0, 3:4, :], mod_ref[0, 4:5, :]).astype(BF16)
    cos = cos_ref[...]
    sin = sin_ref[...]

    q = jnp.dot(h, wqk_ref[:, OFF_Q:OFF_K], preferred_element_type=F32)
    for hd in range(N_Q_HEADS):
        qh = q[:, hd * HEAD_DIM:(hd + 1) * HEAD_DIM]
        q_ref[0, hd] = (_head_norm_rope(qh, qg_ref[...], cos, sin) * Q_PRESCALE).astype(BF16)

    k = jnp.dot(h, wqk_ref[:, OFF_K:OFF_V], preferred_element_type=F32)
    v = jnp.dot(h, wr_ref[:, 0:KV_WIDTH], preferred_element_type=F32)
    for hd in range(N_KV_HEADS):
        head = slice(hd * HEAD_DIM, (hd + 1) * HEAD_DIM)
        k_ref[0, hd] = _head_norm_rope(k[:, head], kg_ref[...], cos, sin).astype(BF16)
        v_ref[0, hd] = v[:, head].astype(BF16)

    lx_ref[0] = jnp.dot(h, wr_ref[:, OFF_LX - OFF_V:OFF_LG - OFF_V], preferred_element_type=F32)
    lg_ref[0] = jnp.dot(h, wr_ref[:, OFF_LG - OFF_V:OFF_GATES - OFF_V], preferred_element_type=F32)
    gates_ref[0] = jnp.dot(h, wr_ref[:, OFF_GATES - OFF_V:IN_COLS - OFF_V], preferred_element_type=F32)


def _proj_call(x, mod, g, w_qk, w_rest, l, q_g, k_g, cos, sin):
    b, s, d = x.shape
    tm = PROJ_TM
    const2 = lambda bi, i: (0, 0)
    return pl.pallas_call(
        _proj_kernel,
        grid=(b, s // tm),
        in_specs=[
            pl.BlockSpec((1, tm, d), lambda bi, i: (bi, i, 0)),
            pl.BlockSpec((1, N_MOD, d), lambda bi, i: (bi, 0, 0)),
            pl.BlockSpec((1, d), const2),
            _layer_weight_spec(w_qk, l),
            _layer_weight_spec(w_rest, l),
            pl.BlockSpec((1, HEAD_DIM), const2),
            pl.BlockSpec((1, HEAD_DIM), const2),
            pl.BlockSpec((tm, HEAD_DIM), lambda bi, i: (i, 0)),
            pl.BlockSpec((tm, HEAD_DIM), lambda bi, i: (i, 0)),
        ],
        out_specs=[
            pl.BlockSpec((1, N_Q_HEADS, tm, HEAD_DIM), lambda bi, i: (bi, 0, i, 0)),
            pl.BlockSpec((1, N_KV_HEADS, tm, HEAD_DIM), lambda bi, i: (bi, 0, i, 0)),
            pl.BlockSpec((1, N_KV_HEADS, tm, HEAD_DIM), lambda bi, i: (bi, 0, i, 0)),
            pl.BlockSpec((1, tm, LRU_WIDTH), lambda bi, i: (bi, i, 0)),
            pl.BlockSpec((1, tm, LRU_WIDTH), lambda bi, i: (bi, i, 0)),
            pl.BlockSpec((1, tm, 2 * d), lambda bi, i: (bi, i, 0)),
        ],
        out_shape=[
            jax.ShapeDtypeStruct((b, N_Q_HEADS, s, HEAD_DIM), BF16),
            jax.ShapeDtypeStruct((b, N_KV_HEADS, s, HEAD_DIM), BF16),
            jax.ShapeDtypeStruct((b, N_KV_HEADS, s, HEAD_DIM), BF16),
            jax.ShapeDtypeStruct((b, s, LRU_WIDTH), F32),
            jax.ShapeDtypeStruct((b, s, LRU_WIDTH), F32),
            jax.ShapeDtypeStruct((b, s, 2 * d), F32),
        ],
        compiler_params=_cparams(("arbitrary", "arbitrary")),
        name="in_proj",
    )(x, mod, g.reshape(1, d), w_qk, w_rest, q_g.reshape(1, HEAD_DIM), k_g.reshape(1, HEAD_DIM), cos, sin)


def _attn_kernel(q_ref, k_ref, v_ref, o_ref, s_scr, m_scr, vext_s):
    n_kc = k_ref.shape[2] // ATT_KC
    row_blocks = ATT_TQ // ATT_M
    n_items = GROUP * row_blocks

    vext_s[:, :HEAD_DIM] = v_ref[0, 0]
    vext_s[:, HEAD_DIM:] = jnp.ones((vext_s.shape[0], HEAD_DIM), BF16)

    def item_rows(item):
        g = item // row_blocks
        r0 = pl.multiple_of((item % row_blocks) * ATT_M, ATT_M)
        return g, pl.ds(r0, ATT_M)

    def scores(item, slot):
        g, rows = item_rows(item)
        q = q_ref[0, g, rows, :]
        m = None
        for c in range(n_kc):
            kc = k_ref[0, 0, c * ATT_KC:(c + 1) * ATT_KC, :]
            s = lax.dot_general(q, kc, (((1,), (1,)), ((), ())), preferred_element_type=F32)
            s_scr[slot, :, c * ATT_KC:(c + 1) * ATT_KC] = s
            mc = jnp.max(s, axis=-1, keepdims=True)
            m = mc if m is None else jnp.maximum(m, mc)
        m_scr[slot] = m

    def finish(item, slot):
        g, rows = item_rows(item)
        m = m_scr[slot]
        acc = jnp.zeros((ATT_M, 2 * HEAD_DIM), F32)
        for c in range(n_kc):
            p = jnp.exp2(s_scr[slot, :, c * ATT_KC:(c + 1) * ATT_KC] - m)
            vc = vext_s[c * ATT_KC:(c + 1) * ATT_KC, :]
            acc = acc + jnp.dot(p.astype(BF16), vc, preferred_element_type=F32)
        o_ref[0, g, rows, :] = (acc[:, :HEAD_DIM] / acc[:, HEAD_DIM:]).astype(BF16)

    scores(jnp.int32(0), 0)

    def group_body(j, carry):
        first = ATT_UNROLL * j
        for u in range(ATT_UNROLL):
            scores(jnp.minimum(first + u + 1, n_items - 1), (u + 1) % 2)
            finish(first + u, u % 2)
        return carry

    lax.fori_loop(0, n_items // ATT_UNROLL, group_body, 0)


def _attn_call(q, k, v):
    b, _, s, _ = q.shape
    return pl.pallas_call(
        _attn_kernel,
        grid=(b, N_KV_HEADS, s // ATT_TQ),
        in_specs=[
            pl.BlockSpec((1, GROUP, ATT_TQ, HEAD_DIM), lambda bi, kh, i: (bi, kh, i, 0)),
            pl.BlockSpec((1, 1, s, HEAD_DIM), lambda bi, kh, i: (bi, kh, 0, 0)),
            pl.BlockSpec((1, 1, s, HEAD_DIM), lambda bi, kh, i: (bi, kh, 0, 0)),
        ],
        out_specs=pl.BlockSpec((1, GROUP, ATT_TQ, HEAD_DIM), lambda bi, kh, i: (bi, kh, i, 0)),
        out_shape=jax.ShapeDtypeStruct((b, N_Q_HEADS, s, HEAD_DIM), BF16),
        scratch_shapes=[pltpu.VMEM((2, ATT_M, s), F32), pltpu.VMEM((2, ATT_M, 1), F32),
                        pltpu.VMEM((s, 2 * HEAD_DIM), BF16)],
        compiler_params=_cparams(("arbitrary", "arbitrary", "arbitrary")),
        name="attention",
    )(q, k, v)


def _gelu_tanh(x):
    c = math.sqrt(2.0 / math.pi)
    half_x = 0.5 * x
    return half_x + half_x * jnp.tanh(x * (c + (0.044715 * c) * (x * x)))


def _lru_gates(xb, k, wg_ref, ba_ref, bx_ref, lam_ref):
    cols = slice(k * LRU_BLOCK_W, (k + 1) * LRU_BLOCK_W)
    pre = jnp.dot(xb.astype(BF16), wg_ref[k], preferred_element_type=F32)
    r = _sigmoid(pre[:, :LRU_BLOCK_W] + ba_ref[:, cols])
    ig = _sigmoid(pre[:, LRU_BLOCK_W:] + bx_ref[:, cols])
    z = -lam_ref[:, cols]
    softplus = jnp.maximum(z, 0.0) + jnp.log1p(jnp.exp(-jnp.abs(z)))
    a = jnp.exp2(r * ((-LRU_C * math.log2(math.e)) * softplus))
    one_m_a2 = 1.0 - a * a
    root = jnp.where(one_m_a2 > 0.0, one_m_a2 * lax.rsqrt(one_m_a2), 0.0)
    return a, root * (ig * xb)


def _lru_scan_block(a_s, u_s, hc_s, k, nb, reverse):
    h = hc_s[k]
    for t in (reversed(range(LRU_TC)) if reverse else range(LRU_TC)):
        rows = slice(t * nb, (t + 1) * nb)
        h = a_s[k, rows, :] * h + u_s[k, rows, :]
        u_s[k, rows, :] = h
    hc_s[k] = h


def _lru_fwd_kernel(x_ref, prev_ref, next_ref, cw_ref, cb_ref, wg_ref, ba_ref, bx_ref, lam_ref,
                    hf_ref, xc_ref, bm_s, xt_s, u_s, hc_s):
    nb = x_ref.shape[0]
    tc, pitch, halo = LRU_TC, LRU_PITCH, LRU_HALO
    ext = tc + 2 * halo
    n_taps = cw_ref.shape[0]
    chunk = pl.program_id(0)
    n = pl.num_programs(0)

    @pl.when(chunk == 0)
    def _():
        hc_s[...] = jnp.zeros_like(hc_s)

    for k in range(LRU_BLOCKS):
        cols = slice(k * LRU_BLOCK_W, (k + 1) * LRU_BLOCK_W)
        for b in range(nb):
            base = b * pitch
            bm_s[k, base:base + halo, :] = jnp.where(chunk == 0, 0.0, prev_ref[b, :, cols])
            bm_s[k, base + halo:base + halo + tc, :] = x_ref[b, :, cols]
            bm_s[k, base + halo + tc:base + ext, :] = jnp.where(chunk == n - 1, 0.0, next_ref[b, :, cols])
        for tau in range(halo - 2, halo + tc + n_taps - 2):
            xt_s[k, tau * nb:(tau + 1) * nb, :] = bm_s[k, pl.ds(tau, nb, stride=pitch), :]
        xb = cb_ref[:, cols]
        for j in range(n_taps):
            r0 = (halo - 2 + j) * nb
            xb = xb + xt_s[k, r0:r0 + tc * nb, :] * cw_ref[j:j + 1, cols]
        xc_ref[0, k] = xb
        a, u = _lru_gates(xb, k, wg_ref, ba_ref, bx_ref, lam_ref)
        bm_s[k, 0:tc * nb, :] = a
        u_s[k] = u
        _lru_scan_block(bm_s, u_s, hc_s, k, nb, reverse=False)
        for b in range(nb):
            hf_ref[b, :, cols] = u_s[k, pl.ds(b, tc, stride=nb), :]


def _lru_bwd_kernel(xc_ref, wg_ref, ba_ref, bx_ref, lam_ref, hf_ref, lg_ref, o_ref, a_s, u_s, hc_s):
    nb = hf_ref.shape[0]

    @pl.when(pl.program_id(0) == 0)
    def _():
        hc_s[...] = jnp.zeros_like(hc_s)

    for k in range(LRU_BLOCKS):
        cols = slice(k * LRU_BLOCK_W, (k + 1) * LRU_BLOCK_W)
        a, u = _lru_gates(xc_ref[0, k], k, wg_ref, ba_ref, bx_ref, lam_ref)
        a_s[k] = a
        u_s[k] = u
        _lru_scan_block(a_s, u_s, hc_s, k, nb, reverse=True)
        for b in range(nb):
            hk = u_s[k, pl.ds(b, LRU_TC, stride=nb), :]
            o_ref[b, :, cols] = ((hf_ref[b, :, cols] + hk) * _gelu_tanh(lg_ref[b, :, cols])).astype(BF16)


def _lru_call(lx, lg, conv_w, conv_b, w_gate, b_a, b_x, lam):
    b, s, w = lx.shape
    tc, halo = LRU_TC, LRU_HALO
    n = s // tc
    hb_per_chunk = tc // halo
    last_hb = s // halo - 1
    const2 = lambda i: (0, 0)
    const3 = lambda i: (0, 0, 0)
    slab_rows = b * tc
    vec = lambda a, d: a[d].reshape(1, w)
    gate_specs = [pl.BlockSpec(w_gate.shape[1:], const3)] + [pl.BlockSpec((1, w), const2)] * 3
    slab = lambda rows: pltpu.VMEM((LRU_BLOCKS, rows, LRU_BLOCK_W), F32)

    hf, xc = pl.pallas_call(
        _lru_fwd_kernel,
        grid=(n,),
        in_specs=[
            pl.BlockSpec((b, tc, w), lambda i: (0, i, 0)),
            pl.BlockSpec((b, halo, w), lambda i: (0, jnp.maximum(i * hb_per_chunk - 1, 0), 0)),
            pl.BlockSpec((b, halo, w), lambda i: (0, jnp.minimum((i + 1) * hb_per_chunk, last_hb), 0)),
            pl.BlockSpec(conv_w.shape, const2),
            pl.BlockSpec((1, w), const2),
        ] + gate_specs,
        out_specs=[
            pl.BlockSpec((b, tc, w), lambda i: (0, i, 0)),
            pl.BlockSpec((1, LRU_BLOCKS, slab_rows, LRU_BLOCK_W), lambda i: (i, 0, 0, 0)),
        ],
        out_shape=[
            jax.ShapeDtypeStruct((b, s, w), F32),
            jax.ShapeDtypeStruct((n, LRU_BLOCKS, slab_rows, LRU_BLOCK_W), F32),
        ],
        scratch_shapes=[slab(b * LRU_PITCH), slab(b * (tc + 2 * halo)), slab(slab_rows), slab(b)],
        compiler_params=_cparams(("arbitrary",)),
        name="lru_fwd",
    )(lx, lx, lx, conv_w, conv_b.reshape(1, w), w_gate[0], vec(b_a, 0), vec(b_x, 0), vec(lam, 0))

    rev_chunk = pl.BlockSpec((b, tc, w), lambda i: (0, n - 1 - i, 0))
    return pl.pallas_call(
        _lru_bwd_kernel,
        grid=(n,),
        in_specs=[pl.BlockSpec((1, LRU_BLOCKS, slab_rows, LRU_BLOCK_W), lambda i: (n - 1 - i, 0, 0, 0))]
        + gate_specs + [rev_chunk, rev_chunk],
        out_specs=rev_chunk,
        out_shape=jax.ShapeDtypeStruct((b, s, w), BF16),
        scratch_shapes=[slab(slab_rows), slab(slab_rows), slab(b)],
        compiler_params=_cparams(("arbitrary",)),
        name="lru_bwd",
    )(xc, w_gate[1], vec(b_a, 1), vec(b_x, 1), vec(lam, 1), hf, lg)


def _merge_kernel(attn_ref, lru_ref, gates_ref, x_ref, mod_ref, wa_ref, wl_ref, wo_ref, o_ref, acat_s):
    for hd in range(N_Q_HEADS):
        acat_s[:, hd * HEAD_DIM:(hd + 1) * HEAD_DIM] = attn_ref[0, hd]
    a = jnp.dot(acat_s[...], wa_ref[...], preferred_element_type=F32)
    r = jnp.dot(lru_ref[0], wl_ref[...], preferred_element_type=F32)
    g = _sigmoid(gates_ref[0])
    merged = g[:, :D_MODEL] * a + g[:, D_MODEL:] * r
    y = jnp.dot(merged.astype(BF16), wo_ref[...], preferred_element_type=F32)
    o_ref[0] = x_ref[0] + mod_ref[0, 5:6, :] * y


def _merge_call(attn, lru, gates, x, mod, w_attn_o, w_lru_o, w_out, l):
    b, s, d = x.shape
    tm = MERGE_TM
    tok = lambda bi, i: (bi, i, 0)
    return pl.pallas_call(
        _merge_kernel,
        grid=(b, s // tm),
        in_specs=[
            pl.BlockSpec((1, N_Q_HEADS, tm, HEAD_DIM), lambda bi, i: (bi, 0, i, 0)),
            pl.BlockSpec((1, tm, LRU_WIDTH), tok),
            pl.BlockSpec((1, tm, 2 * d), tok),
            pl.BlockSpec((1, tm, d), tok),
            pl.BlockSpec((1, N_MOD, d), lambda bi, i: (bi, 0, 0)),
            _layer_weight_spec(w_attn_o, l),
            _layer_weight_spec(w_lru_o, l),
            _layer_weight_spec(w_out, l),
        ],
        out_specs=pl.BlockSpec((1, tm, d), tok),
        out_shape=jax.ShapeDtypeStruct((b, s, d), F32),
        scratch_shapes=[pltpu.VMEM((tm, ATTN_WIDTH), BF16)],
        compiler_params=_cparams(("arbitrary", "arbitrary")),
        name="merge_out",
    )(attn, lru, gates, x, mod, w_attn_o, w_lru_o, w_out)


def _rope_tables(s):
    rows = s // GRID_W
    row_ids = jnp.broadcast_to(jnp.arange(rows, dtype=F32)[:, None], (rows, GRID_W)).reshape(s)
    col_ids = jnp.broadcast_to(jnp.arange(GRID_W, dtype=F32)[None, :], (rows, GRID_W)).reshape(s)
    axis_dim = HEAD_DIM // 2
    inv_freq = ROPE_THETA ** (-jnp.arange(0, axis_dim, 2, dtype=F32) / axis_dim)
    ang = jnp.concatenate([row_ids[:, None] * inv_freq, col_ids[:, None] * inv_freq], axis=-1)
    cos, sin = jnp.cos(ang), jnp.sin(ang)
    return jnp.concatenate([cos, cos], axis=-1), jnp.concatenate([-sin, sin], axis=-1)


def _deinterleave_heads(w):
    lead = w.shape[:-1]
    n_heads = w.shape[-1] // HEAD_DIM
    w = w.reshape(*lead, n_heads, HEAD_DIM // 2, 2)
    return jnp.swapaxes(w, -1, -2).reshape(*lead, n_heads * HEAD_DIM)


def kernel(x, c, ada_w, ada_b, norm_g, ffn1_up, ffn1_down, w_in, q_norm_g, k_norm_g, conv_w, conv_b,
           lru_wa, lru_ba, lru_wx, lru_bx, lru_lambda, w_attn_o, w_lru_o, w_out, ffn2_up, ffn2_down, final_g):
    b, s, d = x.shape
    depth = ada_w.shape[0]
    mod = _ada_call(c, ada_w, ada_b).reshape(depth, b, N_MOD, d)
    cos, sin = _rope_tables(s)

    ffn1_up, ffn2_up = ffn1_up.astype(BF16), ffn2_up.astype(BF16)
    ffn1_down, ffn2_down = ffn1_down.astype(BF16), ffn2_down.astype(BF16)
    w_qk = _deinterleave_heads(w_in[..., :OFF_V]).astype(BF16)
    w_rest = w_in[..., OFF_V:].astype(BF16)
    q_g, k_g = _deinterleave_heads(q_norm_g), _deinterleave_heads(k_norm_g)
    w_attn_o, w_lru_o, w_out = w_attn_o.astype(BF16), w_lru_o.astype(BF16), w_out.astype(BF16)
    w_gate = jnp.concatenate([lru_wa, lru_wx], axis=-1).astype(BF16)

    for l in range(depth):
        x = _ffn_call(x, mod[l], norm_g[l, 0], ffn1_up, ffn1_down, l, 0)
        q, k, v, lx, lg, gates = _proj_call(x, mod[l], norm_g[l, 1], w_qk, w_rest, l, q_g[l], k_g[l], cos, sin)
        attn = _attn_call(q, k, v)
        lru = _lru_call(lx, lg, conv_w[l], conv_b[l], w_gate[l], lru_ba[l], lru_bx[l], lru_lambda[l])
        x = _merge_call(attn, lru, gates, x, mod[l], w_attn_o, w_lru_o, w_out, l)
        x = _ffn_call(x, mod[l], norm_g[l, 2], ffn2_up, ffn2_down, l, 6,
                      final_g=final_g if l == depth - 1 else None)
    return x
```

```python
import functools
import math

import jax
import jax.numpy as jnp
from jax import lax
from jax.experimental import pallas as pl
from jax.experimental.pallas import tpu as pltpu

F32 = jnp.float32
BF16 = jnp.bfloat16

SUBLANES = 8
VMEM_LIMIT_BYTES = 56 * 1024 * 1024

D_MODEL = 1024
HEAD_DIM = 128
N_Q_HEADS = 8
N_KV_HEADS = 2
GROUP = N_Q_HEADS // N_KV_HEADS
ATTN_WIDTH = N_Q_HEADS * HEAD_DIM
KV_WIDTH = N_KV_HEADS * HEAD_DIM
ROPE_THETA = 10000.0
GRID_W = 64
LRU_WIDTH = D_MODEL
LRU_BLOCKS = 8
LRU_BLOCK_W = LRU_WIDTH // LRU_BLOCKS
LRU_C = 8.0
D_FF = 2816
FFN_RES = 0.5
N_MOD = 9
EPS = 1e-6
Q_PRESCALE = (HEAD_DIM ** -0.5) * math.log2(math.e)

OFF_Q = 0
OFF_K = ATTN_WIDTH
OFF_V = OFF_K + KV_WIDTH
OFF_LX = OFF_V + KV_WIDTH
OFF_LG = OFF_LX + LRU_WIDTH
OFF_GATES = OFF_LG + LRU_WIDTH
IN_COLS = OFF_GATES + 2 * D_MODEL

ADA_TN = 2304
FFN_TM = 1024
FFN_TF = 1024
FFN_CHUNKS = tuple((lo, min(lo + FFN_TF, D_FF)) for lo in range(0, D_FF, FFN_TF))
PROJ_TM = 512
ATT_TQ = 4096
ATT_M = 512
ATT_KC = 512
ATT_UNROLL = 8
LRU_TC = 128
LRU_HALO = SUBLANES
LRU_PITCH = LRU_TC + 2 * LRU_HALO + SUBLANES
MERGE_TM = 512


def _cparams(semantics):
    return pltpu.CompilerParams(dimension_semantics=semantics, vmem_limit_bytes=VMEM_LIMIT_BYTES)


def _sigmoid(x):
    return 1.0 / (1.0 + jnp.exp2(x * (-math.log2(math.e))))


def _rms_modulate(x, g, shift, scale):
    y = x * lax.rsqrt(jnp.mean(x * x, axis=-1, keepdims=True) + EPS) * g
    return y * (1.0 + scale) + shift


def _ada_kernel(c_ref, w_ref, b_ref, o_ref):
    c = c_ref[...]
    c_act = (c * _sigmoid(c)).astype(BF16)
    o_ref[0] = jnp.dot(c_act, w_ref[0].astype(BF16), preferred_element_type=F32) + b_ref[0]


def _ada_call(c, ada_w, ada_b):
    depth, d, n = ada_w.shape
    b = c.shape[0]
    return pl.pallas_call(
        _ada_kernel,
        grid=(depth, n // ADA_TN),
        in_specs=[
            pl.BlockSpec((b, d), lambda l, j: (0, 0)),
            pl.BlockSpec((1, d, ADA_TN), lambda l, j: (l, 0, j)),
            pl.BlockSpec((1, 1, ADA_TN), lambda l, j: (l, 0, j)),
        ],
        out_specs=pl.BlockSpec((1, b, ADA_TN), lambda l, j: (l, 0, j)),
        out_shape=jax.ShapeDtypeStruct((depth, b, n), F32),
        compiler_params=_cparams(("arbitrary", "arbitrary")),
        name="ada_mod",
    )(c, ada_w, ada_b.reshape(depth, 1, n))


def _ffn_kernel(*refs, mod_base, final_norm):
    if final_norm:
        x_ref, mod_ref, g_ref, wup_ref, wd_ref, fg_ref, o_ref, acc_s = refs
    else:
        x_ref, mod_ref, g_ref, wup_ref, wd_ref, o_ref, acc_s = refs
    shift = mod_ref[0, mod_base:mod_base + 1, :]
    scale = mod_ref[0, mod_base + 1:mod_base + 2, :]
    h = _rms_modulate(x_ref[0], g_ref[...], shift, scale).astype(BF16)
    for n, (lo, hi) in enumerate(FFN_CHUNKS):
        gate = jnp.dot(h, wup_ref[:, lo:hi], preferred_element_type=F32)
        up = jnp.dot(h, wup_ref[:, D_FF + lo:D_FF + hi], preferred_element_type=F32)
        act = (gate * _sigmoid(gate) * up).astype(BF16)
        part = jnp.dot(act, wd_ref[lo:hi, :], preferred_element_type=F32)
        if n == 0:
            acc_s[...] = part
        else:
            acc_s[...] += part
    res_gate = mod_ref[0, mod_base + 2:mod_base + 3, :]
    out = x_ref[0] + (FFN_RES * res_gate) * acc_s[...]
    if final_norm:
        out = out * lax.rsqrt(jnp.mean(out * out, axis=-1, keepdims=True) + EPS) * fg_ref[...]
    o_ref[0] = out


def _layer_weight_spec(w, l):
    return pl.BlockSpec((None,) + w.shape[1:], lambda *_: (l, 0, 0), pipeline_mode=pl.Buffered(1))


def _ffn_call(x, mod, g, w_up, w_down, l, mod_base, final_g=None):
    b, s, d = x.shape
    final_norm = final_g is not None
    const2 = lambda bi, i: (0, 0)
    in_specs = [
        pl.BlockSpec((1, FFN_TM, d), lambda bi, i: (bi, i, 0)),
        pl.BlockSpec((1, N_MOD, d), lambda bi, i: (bi, 0, 0)),
        pl.BlockSpec((1, d), const2),
        _layer_weight_spec(w_up, l),
        _layer_weight_spec(w_down, l),
    ]
    args = [x, mod, g.reshape(1, d), w_up, w_down]
    if final_norm:
        in_specs.append(pl.BlockSpec((1, d), const2))
        args.append(final_g.reshape(1, d))
    return pl.pallas_call(
        functools.partial(_ffn_kernel, mod_base=mod_base, final_norm=final_norm),
        grid=(b, s // FFN_TM),
        in_specs=in_specs,
        out_specs=pl.BlockSpec((1, FFN_TM, d), lambda bi, i: (bi, i, 0)),
        out_shape=jax.ShapeDtypeStruct((b, s, d), F32),
        scratch_shapes=[pltpu.VMEM((FFN_TM, d), F32)],
        compiler_params=_cparams(("arbitrary", "arbitrary")),
        name="ffn_final" if final_norm else "ffn",
    )(*args)


def _head_norm_rope(xh, g, cos, sin):
    y = xh * lax.rsqrt(jnp.mean(xh * xh, axis=-1, keepdims=True) + EPS) * g
    return y * cos + pltpu.roll(y, HEAD_DIM // 2, 1) * sin


def _proj_kernel(x_ref, mod_ref, g_ref, wqk_ref, wr_ref, qg_ref, kg_ref, cos_ref, sin_ref,
                 q_ref, k_ref, v_ref, lx_ref, lg_ref, gates_ref):
    h = _rms_modulate(x_ref[0], g_ref[...], mod_ref[0, 3:4, :], mod_ref[0, 4:5, :]).astype(BF16)
    cos = cos_ref[...]
    sin = sin_ref[...]

    q = jnp.dot(h, wqk_ref[:, OFF_Q:OFF_K], preferred_element_type=F32)
    for hd in range(N_Q_HEADS):
        qh = q[:, hd * HEAD_DIM:(hd + 1) * HEAD_DIM]
        q_ref[0, hd] = (_head_norm_rope(qh, qg_ref[...], cos, sin) * Q_PRESCALE).astype(BF16)

    k = jnp.dot(h, wqk_ref[:, OFF_K:OFF_V], preferred_element_type=F32)
    v = jnp.dot(h, wr_ref[:, 0:KV_WIDTH], preferred_element_type=F32)
    for hd in range(N_KV_HEADS):
        head = slice(hd * HEAD_DIM, (hd + 1) * HEAD_DIM)
        k_ref[0, hd] = _head_norm_rope(k[:, head], kg_ref[...], cos, sin).astype(BF16)
        v_ref[0, hd] = v[:, head].astype(BF16)

    lx_ref[0] = jnp.dot(h, wr_ref[:, OFF_LX - OFF_V:OFF_LG - OFF_V], preferred_element_type=F32)
    lg_ref[0] = _gelu_tanh(jnp.dot(h, wr_ref[:, OFF_LG - OFF_V:OFF_GATES - OFF_V], preferred_element_type=F32))
    gates_ref[0] = jnp.dot(h, wr_ref[:, OFF_GATES - OFF_V:IN_COLS - OFF_V], preferred_element_type=F32)


def _proj_call(x, mod, g, w_qk, w_rest, l, q_g, k_g, cos, sin):
    b, s, d = x.shape
    tm = PROJ_TM
    const2 = lambda bi, i: (0, 0)
    return pl.pallas_call(
        _proj_kernel,
        grid=(b, s // tm),
        in_specs=[
            pl.BlockSpec((1, tm, d), lambda bi, i: (bi, i, 0)),
            pl.BlockSpec((1, N_MOD, d), lambda bi, i: (bi, 0, 0)),
            pl.BlockSpec((1, d), const2),
            _layer_weight_spec(w_qk, l),
            _layer_weight_spec(w_rest, l),
            pl.BlockSpec((1, HEAD_DIM), const2),
            pl.BlockSpec((1, HEAD_DIM), const2),
            pl.BlockSpec((tm, HEAD_DIM), lambda bi, i: (i, 0)),
            pl.BlockSpec((tm, HEAD_DIM), lambda bi, i: (i, 0)),
        ],
        out_specs=[
            pl.BlockSpec((1, N_Q_HEADS, tm, HEAD_DIM), lambda bi, i: (bi, 0, i, 0)),
            pl.BlockSpec((1, N_KV_HEADS, tm, HEAD_DIM), lambda bi, i: (bi, 0, i, 0)),
            pl.BlockSpec((1, N_KV_HEADS, tm, HEAD_DIM), lambda bi, i: (bi, 0, i, 0)),
            pl.BlockSpec((1, tm, LRU_WIDTH), lambda bi, i: (bi, i, 0)),
            pl.BlockSpec((1, tm, LRU_WIDTH), lambda bi, i: (bi, i, 0)),
            pl.BlockSpec((1, tm, 2 * d), lambda bi, i: (bi, i, 0)),
        ],
        out_shape=[
            jax.ShapeDtypeStruct((b, N_Q_HEADS, s, HEAD_DIM), BF16),
            jax.ShapeDtypeStruct((b, N_KV_HEADS, s, HEAD_DIM), BF16),
            jax.ShapeDtypeStruct((b, N_KV_HEADS, s, HEAD_DIM), BF16),
            jax.ShapeDtypeStruct((b, s, LRU_WIDTH), F32),
            jax.ShapeDtypeStruct((b, s, LRU_WIDTH), F32),
            jax.ShapeDtypeStruct((b, s, 2 * d), F32),
        ],
        compiler_params=_cparams(("arbitrary", "arbitrary")),
        name="in_proj",
    )(x, mod, g.reshape(1, d), w_qk, w_rest, q_g.reshape(1, HEAD_DIM), k_g.reshape(1, HEAD_DIM), cos, sin)


def _attn_kernel(q_ref, k_ref, v_ref, o_ref, s_scr, m_scr, vext_s):
    n_kc = k_ref.shape[2] // ATT_KC
    row_blocks = ATT_TQ // ATT_M
    n_items = GROUP * row_blocks

    vext_s[:, :HEAD_DIM] = v_ref[0, 0]
    vext_s[:, HEAD_DIM:] = jnp.ones((vext_s.shape[0], HEAD_DIM), BF16)

    def item_rows(item):
        g = item // row_blocks
        r0 = pl.multiple_of((item % row_blocks) * ATT_M, ATT_M)
        return g, pl.ds(r0, ATT_M)

    def scores(item, slot):
        g, rows = item_rows(item)
        q = q_ref[0, g, rows, :]
        m = None
        for c in range(n_kc):
            kc = k_ref[0, 0, c * ATT_KC:(c + 1) * ATT_KC, :]
            s = lax.dot_general(q, kc, (((1,), (1,)), ((), ())), preferred_element_type=F32)
            s_scr[slot, :, c * ATT_KC:(c + 1) * ATT_KC] = s
            mc = jnp.max(s, axis=-1, keepdims=True)
            m = mc if m is None else jnp.maximum(m, mc)
        m_scr[slot] = m

    def finish(item, slot):
        g, rows = item_rows(item)
        m = m_scr[slot]
        acc = jnp.zeros((ATT_M, 2 * HEAD_DIM), F32)
        for c in range(n_kc):
            p = jnp.exp2(s_scr[slot, :, c * ATT_KC:(c + 1) * ATT_KC] - m)
            vc = vext_s[c * ATT_KC:(c + 1) * ATT_KC, :]
            acc = acc + jnp.dot(p.astype(BF16), vc, preferred_element_type=F32)
        o_ref[0, g, rows, :] = (acc[:, :HEAD_DIM] / acc[:, HEAD_DIM:]).astype(BF16)

    scores(jnp.int32(0), 0)

    def group_body(j, carry):
        first = ATT_UNROLL * j
        for u in range(ATT_UNROLL):
            scores(jnp.minimum(first + u + 1, n_items - 1), (u + 1) % 2)
            finish(first + u, u % 2)
        return carry

    lax.fori_loop(0, n_items // ATT_UNROLL, group_body, 0)


def _attn_call(q, k, v):
    b, _, s, _ = q.shape
    return pl.pallas_call(
        _attn_kernel,
        grid=(b, N_KV_HEADS, s // ATT_TQ),
        in_specs=[
            pl.BlockSpec((1, GROUP, ATT_TQ, HEAD_DIM), lambda bi, kh, i: (bi, kh, i, 0)),
            pl.BlockSpec((1, 1, s, HEAD_DIM), lambda bi, kh, i: (bi, kh, 0, 0)),
            pl.BlockSpec((1, 1, s, HEAD_DIM), lambda bi, kh, i: (bi, kh, 0, 0)),
        ],
        out_specs=pl.BlockSpec((1, GROUP, ATT_TQ, HEAD_DIM), lambda bi, kh, i: (bi, kh, i, 0)),
        out_shape=jax.ShapeDtypeStruct((b, N_Q_HEADS, s, HEAD_DIM), BF16),
        scratch_shapes=[pltpu.VMEM((2, ATT_M, s), F32), pltpu.VMEM((2, ATT_M, 1), F32),
                        pltpu.VMEM((s, 2 * HEAD_DIM), BF16)],
        compiler_params=_cparams(("arbitrary", "arbitrary", "arbitrary")),
        name="attention",
    )(q, k, v)


def _gelu_tanh(x):
    c = math.sqrt(2.0 / math.pi)
    half_x = 0.5 * x
    return half_x + half_x * jnp.tanh(x * (c + (0.044715 * c) * (x * x)))


def _lru_gates(xb, k, wg_ref, ba_ref, bx_ref, lam_ref):
    cols = slice(k * LRU_BLOCK_W, (k + 1) * LRU_BLOCK_W)
    pre = jnp.dot(xb.astype(BF16), wg_ref[k], preferred_element_type=F32)
    r = _sigmoid(pre[:, :LRU_BLOCK_W] + ba_ref[:, cols])
    ig = _sigmoid(pre[:, LRU_BLOCK_W:] + bx_ref[:, cols])
    z = -lam_ref[:, cols]
    softplus = jnp.maximum(z, 0.0) + jnp.log1p(jnp.exp(-jnp.abs(z)))
    a = jnp.exp2(r * ((-LRU_C * math.log2(math.e)) * softplus))
    one_m_a2 = 1.0 - a * a
    root = jnp.where(one_m_a2 > 0.0, one_m_a2 * lax.rsqrt(one_m_a2), 0.0)
    return a, root * (ig * xb)


def _lru_scan_block(a_s, u_s, hc_s, k, nb, reverse):
    h = hc_s[k]
    for t in (reversed(range(LRU_TC)) if reverse else range(LRU_TC)):
        rows = slice(t * nb, (t + 1) * nb)
        h = a_s[k, rows, :] * h + u_s[k, rows, :]
        u_s[k, rows, :] = h
    hc_s[k] = h


def _lru_fwd_kernel(x_ref, prev_ref, next_ref, cw_ref, cb_ref, wg_ref, ba_ref, bx_ref, lam_ref,
                    hf_ref, xc_ref, bm_s, xt_s, u_s, hc_s):
    nb = x_ref.shape[0]
    tc, pitch, halo = LRU_TC, LRU_PITCH, LRU_HALO
    ext = tc + 2 * halo
    n_taps = cw_ref.shape[0]
    chunk = pl.program_id(0)
    n = pl.num_programs(0)

    @pl.when(chunk == 0)
    def _():
        hc_s[...] = jnp.zeros_like(hc_s)

    for k in range(LRU_BLOCKS):
        cols = slice(k * LRU_BLOCK_W, (k + 1) * LRU_BLOCK_W)
        for b in range(nb):
            base = b * pitch
            bm_s[k, base:base + halo, :] = jnp.where(chunk == 0, 0.0, prev_ref[b, :, cols])
            bm_s[k, base + halo:base + halo + tc, :] = x_ref[b, :, cols]
            bm_s[k, base + halo + tc:base + ext, :] = jnp.where(chunk == n - 1, 0.0, next_ref[b, :, cols])
        for tau in range(halo - 2, halo + tc + n_taps - 2):
            xt_s[k, tau * nb:(tau + 1) * nb, :] = bm_s[k, pl.ds(tau, nb, stride=pitch), :]
        xb = cb_ref[:, cols]
        for j in range(n_taps):
            r0 = (halo - 2 + j) * nb
            xb = xb + xt_s[k, r0:r0 + tc * nb, :] * cw_ref[j:j + 1, cols]
        xc_ref[0, k] = xb
        a, u = _lru_gates(xb, k, wg_ref, ba_ref, bx_ref, lam_ref)
        bm_s[k, 0:tc * nb, :] = a
        u_s[k] = u
        _lru_scan_block(bm_s, u_s, hc_s, k, nb, reverse=False)
        for b in range(nb):
            hf_ref[b, :, cols] = u_s[k, pl.ds(b, tc, stride=nb), :]


def _lru_bwd_kernel(xc_ref, wg_ref, ba_ref, bx_ref, lam_ref, hf_ref, glg_ref, o_ref, a_s, u_s, hc_s):
    nb = hf_ref.shape[0]

    @pl.when(pl.program_id(0) == 0)
    def _():
        hc_s[...] = jnp.zeros_like(hc_s)

    for k in range(LRU_BLOCKS):
        cols = slice(k * LRU_BLOCK_W, (k + 1) * LRU_BLOCK_W)
        a, u = _lru_gates(xc_ref[0, k], k, wg_ref, ba_ref, bx_ref, lam_ref)
        a_s[k] = a
        u_s[k] = u
        _lru_scan_block(a_s, u_s, hc_s, k, nb, reverse=True)
        for b in range(nb):
            hk = u_s[k, pl.ds(b, LRU_TC, stride=nb), :]
            o_ref[b, :, cols] = ((hf_ref[b, :, cols] + hk) * glg_ref[b, :, cols]).astype(BF16)


def _lru_call(lx, glg, conv_w, conv_b, w_gate, b_a, b_x, lam):
    b, s, w = lx.shape
    tc, halo = LRU_TC, LRU_HALO
    n = s // tc
    hb_per_chunk = tc // halo
    last_hb = s // halo - 1
    const2 = lambda i: (0, 0)
    const3 = lambda i: (0, 0, 0)
    slab_rows = b * tc
    vec = lambda a, d: a[d].reshape(1, w)
    gate_specs = [pl.BlockSpec(w_gate.shape[1:], const3)] + [pl.BlockSpec((1, w), const2)] * 3
    slab = lambda rows: pltpu.VMEM((LRU_BLOCKS, rows, LRU_BLOCK_W), F32)

    hf, xc = pl.pallas_call(
        _lru_fwd_kernel,
        grid=(n,),
        in_specs=[
            pl.BlockSpec((b, tc, w), lambda i: (0, i, 0)),
            pl.BlockSpec((b, halo, w), lambda i: (0, jnp.maximum(i * hb_per_chunk - 1, 0), 0)),
            pl.BlockSpec((b, halo, w), lambda i: (0, jnp.minimum((i + 1) * hb_per_chunk, last_hb), 0)),
            pl.BlockSpec(conv_w.shape, const2),
            pl.BlockSpec((1, w), const2),
        ] + gate_specs,
        out_specs=[
            pl.BlockSpec((b, tc, w), lambda i: (0, i, 0)),
            pl.BlockSpec((1, LRU_BLOCKS, slab_rows, LRU_BLOCK_W), lambda i: (i, 0, 0, 0)),
        ],
        out_shape=[
            jax.ShapeDtypeStruct((b, s, w), F32),
            jax.ShapeDtypeStruct((n, LRU_BLOCKS, slab_rows, LRU_BLOCK_W), F32),
        ],
        scratch_shapes=[slab(b * LRU_PITCH), slab(b * (tc + 2 * halo)), slab(slab_rows), slab(b)],
        compiler_params=_cparams(("arbitrary",)),
        name="lru_fwd",
    )(lx, lx, lx, conv_w, conv_b.reshape(1, w), w_gate[0], vec(b_a, 0), vec(b_x, 0), vec(lam, 0))

    rev_chunk = pl.BlockSpec((b, tc, w), lambda i: (0, n - 1 - i, 0))
    return pl.pallas_call(
        _lru_bwd_kernel,
        grid=(n,),
        in_specs=[pl.BlockSpec((1, LRU_BLOCKS, slab_rows, LRU_BLOCK_W), lambda i: (n - 1 - i, 0, 0, 0))]
        + gate_specs + [rev_chunk, rev_chunk],
        out_specs=rev_chunk,
        out_shape=jax.ShapeDtypeStruct((b, s, w), BF16),
        scratch_shapes=[slab(slab_rows), slab(slab_rows), slab(b)],
        compiler_params=_cparams(("arbitrary",)),
        name="lru_bwd",
    )(xc, w_gate[1], vec(b_a, 1), vec(b_x, 1), vec(lam, 1), hf, glg)


def _merge_kernel(attn_ref, lru_ref, gates_ref, x_ref, mod_ref, wa_ref, wl_ref, wo_ref, o_ref, acat_s):
    for hd in range(N_Q_HEADS):
        acat_s[:, hd * HEAD_DIM:(hd + 1) * HEAD_DIM] = attn_ref[0, hd]
    a = jnp.dot(acat_s[...], wa_ref[...], preferred_element_type=F32)
    r = jnp.dot(lru_ref[0], wl_ref[...], preferred_element_type=F32)
    g = _sigmoid(gates_ref[0])
    merged = g[:, :D_MODEL] * a + g[:, D_MODEL:] * r
    y = jnp.dot(merged.astype(BF16), wo_ref[...], preferred_element_type=F32)
    o_ref[0] = x_ref[0] + mod_ref[0, 5:6, :] * y


def _merge_call(attn, lru, gates, x, mod, w_attn_o, w_lru_o, w_out, l):
    b, s, d = x.shape
    tm = MERGE_TM
    tok = lambda bi, i: (bi, i, 0)
    return pl.pallas_call(
        _merge_kernel,
        grid=(b, s // tm),
        in_specs=[
            pl.BlockSpec((1, N_Q_HEADS, tm, HEAD_DIM), lambda bi, i: (bi, 0, i, 0)),
            pl.BlockSpec((1, tm, LRU_WIDTH), tok),
            pl.BlockSpec((1, tm, 2 * d), tok),
            pl.BlockSpec((1, tm, d), tok),
            pl.BlockSpec((1, N_MOD, d), lambda bi, i: (bi, 0, 0)),
            _layer_weight_spec(w_attn_o, l),
            _layer_weight_spec(w_lru_o, l),
            _layer_weight_spec(w_out, l),
        ],
        out_specs=pl.BlockSpec((1, tm, d), tok),
        out_shape=jax.ShapeDtypeStruct((b, s, d), F32),
        scratch_shapes=[pltpu.VMEM((tm, ATTN_WIDTH), BF16)],
        compiler_params=_cparams(("arbitrary", "arbitrary")),
        name="merge_out",
    )(attn, lru, gates, x, mod, w_attn_o, w_lru_o, w_out)


def _rope_tables(s):
    rows = s // GRID_W
    row_ids = jnp.broadcast_to(jnp.arange(rows, dtype=F32)[:, None], (rows, GRID_W)).reshape(s)
    col_ids = jnp.broadcast_to(jnp.arange(GRID_W, dtype=F32)[None, :], (rows, GRID_W)).reshape(s)
    axis_dim = HEAD_DIM // 2
    inv_freq = ROPE_THETA ** (-jnp.arange(0, axis_dim, 2, dtype=F32) / axis_dim)
    ang = jnp.concatenate([row_ids[:, None] * inv_freq, col_ids[:, None] * inv_freq], axis=-1)
    cos, sin = jnp.cos(ang), jnp.sin(ang)
    return jnp.concatenate([cos, cos], axis=-1), jnp.concatenate([-sin, sin], axis=-1)


def _deinterleave_heads(w):
    lead = w.shape[:-1]
    n_heads = w.shape[-1] // HEAD_DIM
    w = w.reshape(*lead, n_heads, HEAD_DIM // 2, 2)
    return jnp.swapaxes(w, -1, -2).reshape(*lead, n_heads * HEAD_DIM)


def kernel(x, c, ada_w, ada_b, norm_g, ffn1_up, ffn1_down, w_in, q_norm_g, k_norm_g, conv_w, conv_b,
           lru_wa, lru_ba, lru_wx, lru_bx, lru_lambda, w_attn_o, w_lru_o, w_out, ffn2_up, ffn2_down, final_g):
    b, s, d = x.shape
    depth = ada_w.shape[0]
    mod = _ada_call(c, ada_w, ada_b).reshape(depth, b, N_MOD, d)
    cos, sin = _rope_tables(s)

    ffn1_up, ffn2_up = ffn1_up.astype(BF16), ffn2_up.astype(BF16)
    ffn1_down, ffn2_down = ffn1_down.astype(BF16), ffn2_down.astype(BF16)
    w_qk = _deinterleave_heads(w_in[..., :OFF_V]).astype(BF16)
    w_rest = w_in[..., OFF_V:].astype(BF16)
    q_g, k_g = _deinterleave_heads(q_norm_g), _deinterleave_heads(k_norm_g)
    w_attn_o, w_lru_o, w_out = w_attn_o.astype(BF16), w_lru_o.astype(BF16), w_out.astype(BF16)
    w_gate = jnp.concatenate([lru_wa, lru_wx], axis=-1).astype(BF16)

    for l in range(depth):
        x = _ffn_call(x, mod[l], norm_g[l, 0], ffn1_up, ffn1_down, l, 0)
        q, k, v, lx, glg, gates = _proj_call(x, mod[l], norm_g[l, 1], w_qk, w_rest, l, q_g[l], k_g[l], cos, sin)
        attn = _attn_call(q, k, v)
        lru = _lru_call(lx, glg, conv_w[l], conv_b[l], w_gate[l], lru_ba[l], lru_bx[l], lru_lambda[l])
        x = _merge_call(attn, lru, gates, x, mod[l], w_attn_o, w_lru_o, w_out, l)
        x = _ffn_call(x, mod[l], norm_g[l, 2], ffn2_up, ffn2_down, l, 6,
                      final_g=final_g if l == depth - 1 else None)
    return x
```

```python
import functools
import math

import jax
import jax.numpy as jnp
from jax import lax
from jax.experimental import pallas as pl
from jax.experimental.pallas import tpu as pltpu

F32 = jnp.float32
BF16 = jnp.bfloat16

SUBLANES = 8
VMEM_LIMIT_BYTES = 56 * 1024 * 1024

D_MODEL = 1024
HEAD_DIM = 128
N_Q_HEADS = 8
N_KV_HEADS = 2
GROUP = N_Q_HEADS // N_KV_HEADS
ATTN_WIDTH = N_Q_HEADS * HEAD_DIM
KV_WIDTH = N_KV_HEADS * HEAD_DIM
ROPE_THETA = 10000.0
GRID_W = 64
LRU_WIDTH = D_MODEL
LRU_BLOCKS = 8
LRU_BLOCK_W = LRU_WIDTH // LRU_BLOCKS
LRU_C = 8.0
D_FF = 2816
FFN_RES = 0.5
N_MOD = 9
EPS = 1e-6
Q_PRESCALE = (HEAD_DIM ** -0.5) * math.log2(math.e)

OFF_Q = 0
OFF_K = ATTN_WIDTH
OFF_V = OFF_K + KV_WIDTH
OFF_LX = OFF_V + KV_WIDTH
OFF_LG = OFF_LX + LRU_WIDTH
OFF_GATES = OFF_LG + LRU_WIDTH
IN_COLS = OFF_GATES + 2 * D_MODEL

ADA_TN = 2304
FFN_TM = 1024
FFN_TF = 1024
FFN_CHUNKS = tuple((lo, min(lo + FFN_TF, D_FF)) for lo in range(0, D_FF, FFN_TF))
PROJ_TM = 512
ATT_TQ = 4096
ATT_M = 512
ATT_KC = 512
ATT_UNROLL = 8
LRU_TC = 128
LRU_HALO = SUBLANES
LRU_PITCH = LRU_TC + 2 * LRU_HALO + SUBLANES
MERGE_TM = 512


def _cparams(semantics):
    return pltpu.CompilerParams(dimension_semantics=semantics, vmem_limit_bytes=VMEM_LIMIT_BYTES)


def _sigmoid(x):
    return 1.0 / (1.0 + jnp.exp2(x * (-math.log2(math.e))))


def _rms_modulate(x, g, shift, scale):
    y = x * lax.rsqrt(jnp.mean(x * x, axis=-1, keepdims=True) + EPS) * g
    return y * (1.0 + scale) + shift


def _ada_kernel(c_ref, w_ref, b_ref, o_ref):
    c = c_ref[...]
    c_act = (c * _sigmoid(c)).astype(BF16)
    o_ref[0] = jnp.dot(c_act, w_ref[0].astype(BF16), preferred_element_type=F32) + b_ref[0]


def _ada_call(c, ada_w, ada_b):
    depth, d, n = ada_w.shape
    b = c.shape[0]
    return pl.pallas_call(
        _ada_kernel,
        grid=(depth, n // ADA_TN),
        in_specs=[
            pl.BlockSpec((b, d), lambda l, j: (0, 0)),
            pl.BlockSpec((1, d, ADA_TN), lambda l, j: (l, 0, j)),
            pl.BlockSpec((1, 1, ADA_TN), lambda l, j: (l, 0, j)),
        ],
        out_specs=pl.BlockSpec((1, b, ADA_TN), lambda l, j: (l, 0, j)),
        out_shape=jax.ShapeDtypeStruct((depth, b, n), F32),
        compiler_params=_cparams(("arbitrary", "arbitrary")),
        name="ada_mod",
    )(c, ada_w, ada_b.reshape(depth, 1, n))


def _ffn_kernel(*refs, mod_base, final_norm):
    if final_norm:
        x_ref, mod_ref, g_ref, wup_ref, wd_ref, fg_ref, o_ref, acc_s = refs
    else:
        x_ref, mod_ref, g_ref, wup_ref, wd_ref, o_ref, acc_s = refs
    shift = mod_ref[0, mod_base:mod_base + 1, :]
    scale = mod_ref[0, mod_base + 1:mod_base + 2, :]
    h = _rms_modulate(x_ref[0], g_ref[...], shift, scale).astype(BF16)
    for n, (lo, hi) in enumerate(FFN_CHUNKS):
        gate = jnp.dot(h, wup_ref[:, lo:hi], preferred_element_type=F32)
        up = jnp.dot(h, wup_ref[:, D_FF + lo:D_FF + hi], preferred_element_type=F32)
        act = (gate * _sigmoid(gate) * up).astype(BF16)
        part = jnp.dot(act, wd_ref[lo:hi, :], preferred_element_type=F32)
        if n == 0:
            acc_s[...] = part
        else:
            acc_s[...] += part
    res_gate = mod_ref[0, mod_base + 2:mod_base + 3, :]
    out = x_ref[0] + (FFN_RES * res_gate) * acc_s[...]
    if final_norm:
        out = out * lax.rsqrt(jnp.mean(out * out, axis=-1, keepdims=True) + EPS) * fg_ref[...]
    o_ref[0] = out


def _layer_weight_spec(w, l):
    return pl.BlockSpec((None,) + w.shape[1:], lambda *_: (l, 0, 0), pipeline_mode=pl.Buffered(1))


def _ffn_call(x, mod, g, w_up, w_down, l, mod_base, final_g=None):
    b, s, d = x.shape
    final_norm = final_g is not None
    const2 = lambda bi, i: (0, 0)
    in_specs = [
        pl.BlockSpec((1, FFN_TM, d), lambda bi, i: (bi, i, 0)),
        pl.BlockSpec((1, N_MOD, d), lambda bi, i: (bi, 0, 0)),
        pl.BlockSpec((1, d), const2),
        _layer_weight_spec(w_up, l),
        _layer_weight_spec(w_down, l),
    ]
    args = [x, mod, g.reshape(1, d), w_up, w_down]
    if final_norm:
        in_specs.append(pl.BlockSpec((1, d), const2))
        args.append(final_g.reshape(1, d))
    return pl.pallas_call(
        functools.partial(_ffn_kernel, mod_base=mod_base, final_norm=final_norm),
        grid=(b, s // FFN_TM),
        in_specs=in_specs,
        out_specs=pl.BlockSpec((1, FFN_TM, d), lambda bi, i: (bi, i, 0)),
        out_shape=jax.ShapeDtypeStruct((b, s, d), F32),
        scratch_shapes=[pltpu.VMEM((FFN_TM, d), F32)],
        compiler_params=_cparams(("arbitrary", "arbitrary")),
        name="ffn_final" if final_norm else "ffn",
    )(*args)


def _head_norm_rope(xh, g, cos, sin):
    y = xh * lax.rsqrt(jnp.mean(xh * xh, axis=-1, keepdims=True) + EPS) * g
    return y * cos + pltpu.roll(y, HEAD_DIM // 2, 1) * sin


def _proj_kernel(x_ref, mod_ref, g_ref, wqk_ref, wr_ref, qg_ref, kg_ref, cos_ref, sin_ref,
                 q_ref, k_ref, v_ref, lx_ref, lg_ref, gates_ref):
    h = _rms_modulate(x_ref[0], g_ref[...], mod_ref[0, 3:4, :], mod_ref[0, 4:5, :]).astype(BF16)
    cos = cos_ref[...]
    sin = sin_ref[...]

    lg_ref[0] = _gelu_tanh(jnp.dot(h, wr_ref[:, OFF_LG - OFF_V:OFF_GATES - OFF_V], preferred_element_type=F32))

    q = jnp.dot(h, wqk_ref[:, OFF_Q:OFF_K], preferred_element_type=F32)
    for hd in range(N_Q_HEADS):
        qh = q[:, hd * HEAD_DIM:(hd + 1) * HEAD_DIM]
        q_ref[0, hd] = (_head_norm_rope(qh, qg_ref[...], cos, sin) * Q_PRESCALE).astype(BF16)

    k = jnp.dot(h, wqk_ref[:, OFF_K:OFF_V], preferred_element_type=F32)
    v = jnp.dot(h, wr_ref[:, 0:KV_WIDTH], preferred_element_type=F32)
    for hd in range(N_KV_HEADS):
        head = slice(hd * HEAD_DIM, (hd + 1) * HEAD_DIM)
        k_ref[0, hd] = _head_norm_rope(k[:, head], kg_ref[...], cos, sin).astype(BF16)
        v_ref[0, hd] = v[:, head].astype(BF16)

    lx_ref[0] = jnp.dot(h, wr_ref[:, OFF_LX - OFF_V:OFF_LG - OFF_V], preferred_element_type=F32)
    gates_ref[0] = jnp.dot(h, wr_ref[:, OFF_GATES - OFF_V:IN_COLS - OFF_V], preferred_element_type=F32)


def _proj_call(x, mod, g, w_qk, w_rest, l, q_g, k_g, cos, sin):
    b, s, d = x.shape
    tm = PROJ_TM
    const2 = lambda bi, i: (0, 0)
    return pl.pallas_call(
        _proj_kernel,
        grid=(b, s // tm),
        in_specs=[
            pl.BlockSpec((1, tm, d), lambda bi, i: (bi, i, 0)),
            pl.BlockSpec((1, N_MOD, d), lambda bi, i: (bi, 0, 0)),
            pl.BlockSpec((1, d), const2),
            _layer_weight_spec(w_qk, l),
            _layer_weight_spec(w_rest, l),
            pl.BlockSpec((1, HEAD_DIM), const2),
            pl.BlockSpec((1, HEAD_DIM), const2),
            pl.BlockSpec((tm, HEAD_DIM), lambda bi, i: (i, 0)),
            pl.BlockSpec((tm, HEAD_DIM), lambda bi, i: (i, 0)),
        ],
        out_specs=[
            pl.BlockSpec((1, N_Q_HEADS, tm, HEAD_DIM), lambda bi, i: (bi, 0, i, 0)),
            pl.BlockSpec((1, N_KV_HEADS, tm, HEAD_DIM), lambda bi, i: (bi, 0, i, 0)),
            pl.BlockSpec((1, N_KV_HEADS, tm, HEAD_DIM), lambda bi, i: (bi, 0, i, 0)),
            pl.BlockSpec((1, tm, LRU_WIDTH), lambda bi, i: (bi, i, 0)),
            pl.BlockSpec((1, tm, LRU_WIDTH), lambda bi, i: (bi, i, 0)),
            pl.BlockSpec((1, tm, 2 * d), lambda bi, i: (bi, i, 0)),
        ],
        out_shape=[
            jax.ShapeDtypeStruct((b, N_Q_HEADS, s, HEAD_DIM), BF16),
            jax.ShapeDtypeStruct((b, N_KV_HEADS, s, HEAD_DIM), BF16),
            jax.ShapeDtypeStruct((b, N_KV_HEADS, s, HEAD_DIM), BF16),
            jax.ShapeDtypeStruct((b, s, LRU_WIDTH), F32),
            jax.ShapeDtypeStruct((b, s, LRU_WIDTH), F32),
            jax.ShapeDtypeStruct((b, s, 2 * d), F32),
        ],
        compiler_params=_cparams(("arbitrary", "arbitrary")),
        name="in_proj",
    )(x, mod, g.reshape(1, d), w_qk, w_rest, q_g.reshape(1, HEAD_DIM), k_g.reshape(1, HEAD_DIM), cos, sin)


def _attn_kernel(q_ref, k_ref, v_ref, o_ref, s_scr, m_scr, vext_s):
    n_kc = k_ref.shape[2] // ATT_KC
    row_blocks = ATT_TQ // ATT_M
    n_items = GROUP * row_blocks

    vext_s[:, :HEAD_DIM] = v_ref[0, 0]
    vext_s[:, HEAD_DIM:] = jnp.ones((vext_s.shape[0], HEAD_DIM), BF16)

    def item_rows(item):
        g = item // row_blocks
        r0 = pl.multiple_of((item % row_blocks) * ATT_M, ATT_M)
        return g, pl.ds(r0, ATT_M)

    def scores(item, slot):
        g, rows = item_rows(item)
        q = q_ref[0, g, rows, :]
        m = None
        for c in range(n_kc):
            kc = k_ref[0, 0, c * ATT_KC:(c + 1) * ATT_KC, :]
            s = lax.dot_general(q, kc, (((1,), (1,)), ((), ())), preferred_element_type=F32)
            s_scr[slot, :, c * ATT_KC:(c + 1) * ATT_KC] = s
            mc = jnp.max(s, axis=-1, keepdims=True)
            m = mc if m is None else jnp.maximum(m, mc)
        m_scr[slot] = m

    def finish(item, slot):
        g, rows = item_rows(item)
        m = m_scr[slot]
        acc = jnp.zeros((ATT_M, 2 * HEAD_DIM), F32)
        for c in range(n_kc):
            p = jnp.exp2(s_scr[slot, :, c * ATT_KC:(c + 1) * ATT_KC] - m)
            vc = vext_s[c * ATT_KC:(c + 1) * ATT_KC, :]
            acc = acc + jnp.dot(p.astype(BF16), vc, preferred_element_type=F32)
        o_ref[0, g, rows, :] = (acc[:, :HEAD_DIM] / acc[:, HEAD_DIM:]).astype(BF16)

    scores(jnp.int32(0), 0)

    def group_body(j, carry):
        first = ATT_UNROLL * j
        for u in range(ATT_UNROLL):
            scores(jnp.minimum(first + u + 1, n_items - 1), (u + 1) % 2)
            finish(first + u, u % 2)
        return carry

    lax.fori_loop(0, n_items // ATT_UNROLL, group_body, 0)


def _attn_call(q, k, v):
    b, _, s, _ = q.shape
    return pl.pallas_call(
        _attn_kernel,
        grid=(b, N_KV_HEADS, s // ATT_TQ),
        in_specs=[
            pl.BlockSpec((1, GROUP, ATT_TQ, HEAD_DIM), lambda bi, kh, i: (bi, kh, i, 0)),
            pl.BlockSpec((1, 1, s, HEAD_DIM), lambda bi, kh, i: (bi, kh, 0, 0)),
            pl.BlockSpec((1, 1, s, HEAD_DIM), lambda bi, kh, i: (bi, kh, 0, 0)),
        ],
        out_specs=pl.BlockSpec((1, GROUP, ATT_TQ, HEAD_DIM), lambda bi, kh, i: (bi, kh, i, 0)),
        out_shape=jax.ShapeDtypeStruct((b, N_Q_HEADS, s, HEAD_DIM), BF16),
        scratch_shapes=[pltpu.VMEM((2, ATT_M, s), F32), pltpu.VMEM((2, ATT_M, 1), F32),
                        pltpu.VMEM((s, 2 * HEAD_DIM), BF16)],
        compiler_params=_cparams(("arbitrary", "arbitrary", "arbitrary")),
        name="attention",
    )(q, k, v)


def _gelu_tanh(x):
    c = math.sqrt(2.0 / math.pi)
    half_x = 0.5 * x
    return half_x + half_x * jnp.tanh(x * (c + (0.044715 * c) * (x * x)))


def _lru_gates(xb, k, wg_ref, ba_ref, bx_ref, lam_ref):
    cols = slice(k * LRU_BLOCK_W, (k + 1) * LRU_BLOCK_W)
    pre = jnp.dot(xb.astype(BF16), wg_ref[k], preferred_element_type=F32)
    r = _sigmoid(pre[:, :LRU_BLOCK_W] + ba_ref[:, cols])
    ig = _sigmoid(pre[:, LRU_BLOCK_W:] + bx_ref[:, cols])
    z = -lam_ref[:, cols]
    softplus = jnp.maximum(z, 0.0) + jnp.log1p(jnp.exp(-jnp.abs(z)))
    a = jnp.exp2(r * ((-LRU_C * math.log2(math.e)) * softplus))
    one_m_a2 = 1.0 - a * a
    root = jnp.where(one_m_a2 > 0.0, one_m_a2 * lax.rsqrt(one_m_a2), 0.0)
    return a, root * (ig * xb)


def _lru_scan_block(a_s, u_s, hc_s, k, nb, reverse):
    h = hc_s[k]
    for t in (reversed(range(LRU_TC)) if reverse else range(LRU_TC)):
        rows = slice(t * nb, (t + 1) * nb)
        h = a_s[k, rows, :] * h + u_s[k, rows, :]
        u_s[k, rows, :] = h
    hc_s[k] = h


def _lru_fwd_kernel(x_ref, prev_ref, next_ref, cw_ref, cb_ref, wg_ref, ba_ref, bx_ref, lam_ref,
                    hf_ref, xc_ref, bm_s, xt_s, u_s, hc_s):
    nb = x_ref.shape[0]
    tc, pitch, halo = LRU_TC, LRU_PITCH, LRU_HALO
    ext = tc + 2 * halo
    n_taps = cw_ref.shape[0]
    chunk = pl.program_id(0)
    n = pl.num_programs(0)

    @pl.when(chunk == 0)
    def _():
        hc_s[...] = jnp.zeros_like(hc_s)

    for k in range(LRU_BLOCKS):
        cols = slice(k * LRU_BLOCK_W, (k + 1) * LRU_BLOCK_W)
        for b in range(nb):
            base = b * pitch
            bm_s[k, base:base + halo, :] = jnp.where(chunk == 0, 0.0, prev_ref[b, :, cols])
            bm_s[k, base + halo:base + halo + tc, :] = x_ref[b, :, cols]
            bm_s[k, base + halo + tc:base + ext, :] = jnp.where(chunk == n - 1, 0.0, next_ref[b, :, cols])
        for tau in range(halo - 2, halo + tc + n_taps - 2):
            xt_s[k, tau * nb:(tau + 1) * nb, :] = bm_s[k, pl.ds(tau, nb, stride=pitch), :]
        xb = cb_ref[:, cols]
        for j in range(n_taps):
            r0 = (halo - 2 + j) * nb
            xb = xb + xt_s[k, r0:r0 + tc * nb, :] * cw_ref[j:j + 1, cols]
        xc_ref[0, k] = xb
        a, u = _lru_gates(xb, k, wg_ref, ba_ref, bx_ref, lam_ref)
        bm_s[k, 0:tc * nb, :] = a
        u_s[k] = u
        _lru_scan_block(bm_s, u_s, hc_s, k, nb, reverse=False)
        for b in range(nb):
            hf_ref[b, :, cols] = u_s[k, pl.ds(b, tc, stride=nb), :]


def _lru_bwd_kernel(xc_ref, wg_ref, ba_ref, bx_ref, lam_ref, hf_ref, glg_ref, o_ref, a_s, u_s, hc_s):
    nb = hf_ref.shape[0]

    @pl.when(pl.program_id(0) == 0)
    def _():
        hc_s[...] = jnp.zeros_like(hc_s)

    for k in range(LRU_BLOCKS):
        cols = slice(k * LRU_BLOCK_W, (k + 1) * LRU_BLOCK_W)
        a, u = _lru_gates(xc_ref[0, k], k, wg_ref, ba_ref, bx_ref, lam_ref)
        a_s[k] = a
        u_s[k] = u
        _lru_scan_block(a_s, u_s, hc_s, k, nb, reverse=True)
        for b in range(nb):
            hk = u_s[k, pl.ds(b, LRU_TC, stride=nb), :]
            o_ref[b, :, cols] = ((hf_ref[b, :, cols] + hk) * glg_ref[b, :, cols]).astype(BF16)


def _lru_call(lx, glg, conv_w, conv_b, w_gate, b_a, b_x, lam):
    b, s, w = lx.shape
    tc, halo = LRU_TC, LRU_HALO
    n = s // tc
    hb_per_chunk = tc // halo
    last_hb = s // halo - 1
    const2 = lambda i: (0, 0)
    const3 = lambda i: (0, 0, 0)
    slab_rows = b * tc
    vec = lambda a, d: a[d].reshape(1, w)
    gate_specs = [pl.BlockSpec(w_gate.shape[1:], const3)] + [pl.BlockSpec((1, w), const2)] * 3
    slab = lambda rows: pltpu.VMEM((LRU_BLOCKS, rows, LRU_BLOCK_W), F32)

    hf, xc = pl.pallas_call(
        _lru_fwd_kernel,
        grid=(n,),
        in_specs=[
            pl.BlockSpec((b, tc, w), lambda i: (0, i, 0)),
            pl.BlockSpec((b, halo, w), lambda i: (0, jnp.maximum(i * hb_per_chunk - 1, 0), 0)),
            pl.BlockSpec((b, halo, w), lambda i: (0, jnp.minimum((i + 1) * hb_per_chunk, last_hb), 0)),
            pl.BlockSpec(conv_w.shape, const2),
            pl.BlockSpec((1, w), const2),
        ] + gate_specs,
        out_specs=[
            pl.BlockSpec((b, tc, w), lambda i: (0, i, 0)),
            pl.BlockSpec((1, LRU_BLOCKS, slab_rows, LRU_BLOCK_W), lambda i: (i, 0, 0, 0)),
        ],
        out_shape=[
            jax.ShapeDtypeStruct((b, s, w), F32),
            jax.ShapeDtypeStruct((n, LRU_BLOCKS, slab_rows, LRU_BLOCK_W), F32),
        ],
        scratch_shapes=[slab(b * LRU_PITCH), slab(b * (tc + 2 * halo)), slab(slab_rows), slab(b)],
        compiler_params=_cparams(("arbitrary",)),
        name="lru_fwd",
    )(lx, lx, lx, conv_w, conv_b.reshape(1, w), w_gate[0], vec(b_a, 0), vec(b_x, 0), vec(lam, 0))

    rev_chunk = pl.BlockSpec((b, tc, w), lambda i: (0, n - 1 - i, 0))
    return pl.pallas_call(
        _lru_bwd_kernel,
        grid=(n,),
        in_specs=[pl.BlockSpec((1, LRU_BLOCKS, slab_rows, LRU_BLOCK_W), lambda i: (n - 1 - i, 0, 0, 0))]
        + gate_specs + [rev_chunk, rev_chunk],
        out_specs=rev_chunk,
        out_shape=jax.ShapeDtypeStruct((b, s, w), BF16),
        scratch_shapes=[slab(slab_rows), slab(slab_rows), slab(b)],
        compiler_params=_cparams(("arbitrary",)),
        name="lru_bwd",
    )(xc, w_gate[1], vec(b_a, 1), vec(b_x, 1), vec(lam, 1), hf, glg)


def _merge_kernel(attn_ref, lru_ref, gates_ref, x_ref, mod_ref, wa_ref, wl_ref, wo_ref, o_ref, acat_s):
    for hd in range(N_Q_HEADS):
        acat_s[:, hd * HEAD_DIM:(hd + 1) * HEAD_DIM] = attn_ref[0, hd]
    a = jnp.dot(acat_s[...], wa_ref[...], preferred_element_type=F32)
    r = jnp.dot(lru_ref[0], wl_ref[...], preferred_element_type=F32)
    g = _sigmoid(gates_ref[0])
    merged = g[:, :D_MODEL] * a + g[:, D_MODEL:] * r
    y = jnp.dot(merged.astype(BF16), wo_ref[...], preferred_element_type=F32)
    o_ref[0] = x_ref[0] + mod_ref[0, 5:6, :] * y


def _merge_call(attn, lru, gates, x, mod, w_attn_o, w_lru_o, w_out, l):
    b, s, d = x.shape
    tm = MERGE_TM
    tok = lambda bi, i: (bi, i, 0)
    return pl.pallas_call(
        _merge_kernel,
        grid=(b, s // tm),
        in_specs=[
            pl.BlockSpec((1, N_Q_HEADS, tm, HEAD_DIM), lambda bi, i: (bi, 0, i, 0)),
            pl.BlockSpec((1, tm, LRU_WIDTH), tok),
            pl.BlockSpec((1, tm, 2 * d), tok),
            pl.BlockSpec((1, tm, d), tok),
            pl.BlockSpec((1, N_MOD, d), lambda bi, i: (bi, 0, 0)),
            _layer_weight_spec(w_attn_o, l),
            _layer_weight_spec(w_lru_o, l),
            _layer_weight_spec(w_out, l),
        ],
        out_specs=pl.BlockSpec((1, tm, d), tok),
        out_shape=jax.ShapeDtypeStruct((b, s, d), F32),
        scratch_shapes=[pltpu.VMEM((tm, ATTN_WIDTH), BF16)],
        compiler_params=_cparams(("arbitrary", "arbitrary")),
        name="merge_out",
    )(attn, lru, gates, x, mod, w_attn_o, w_lru_o, w_out)


def _rope_tables(s):
    rows = s // GRID_W
    row_ids = jnp.broadcast_to(jnp.arange(rows, dtype=F32)[:, None], (rows, GRID_W)).reshape(s)
    col_ids = jnp.broadcast_to(jnp.arange(GRID_W, dtype=F32)[None, :], (rows, GRID_W)).reshape(s)
    axis_dim = HEAD_DIM // 2
    inv_freq = ROPE_THETA ** (-jnp.arange(0, axis_dim, 2, dtype=F32) / axis_dim)
    ang = jnp.concatenate([row_ids[:, None] * inv_freq, col_ids[:, None] * inv_freq], axis=-1)
    cos, sin = jnp.cos(ang), jnp.sin(ang)
    return jnp.concatenate([cos, cos], axis=-1), jnp.concatenate([-sin, sin], axis=-1)


def _deinterleave_heads(w):
    lead = w.shape[:-1]
    n_heads = w.shape[-1] // HEAD_DIM
    w = w.reshape(*lead, n_heads, HEAD_DIM // 2, 2)
    return jnp.swapaxes(w, -1, -2).reshape(*lead, n_heads * HEAD_DIM)


def kernel(x, c, ada_w, ada_b, norm_g, ffn1_up, ffn1_down, w_in, q_norm_g, k_norm_g, conv_w, conv_b,
           lru_wa, lru_ba, lru_wx, lru_bx, lru_lambda, w_attn_o, w_lru_o, w_out, ffn2_up, ffn2_down, final_g):
    b, s, d = x.shape
    depth = ada_w.shape[0]
    mod = _ada_call(c, ada_w, ada_b).reshape(depth, b, N_MOD, d)
    cos, sin = _rope_tables(s)

    ffn1_up, ffn2_up = ffn1_up.astype(BF16), ffn2_up.astype(BF16)
    ffn1_down, ffn2_down = ffn1_down.astype(BF16), ffn2_down.astype(BF16)
    w_qk = _deinterleave_heads(w_in[..., :OFF_V]).astype(BF16)
    w_rest = w_in[..., OFF_V:].astype(BF16)
    q_g, k_g = _deinterleave_heads(q_norm_g), _deinterleave_heads(k_norm_g)
    w_attn_o, w_lru_o, w_out = w_attn_o.astype(BF16), w_lru_o.astype(BF16), w_out.astype(BF16)
    w_gate = jnp.concatenate([lru_wa, lru_wx], axis=-1).astype(BF16)

    for l in range(depth):
        x = _ffn_call(x, mod[l], norm_g[l, 0], ffn1_up, ffn1_down, l, 0)
        q, k, v, lx, glg, gates = _proj_call(x, mod[l], norm_g[l, 1], w_qk, w_rest, l, q_g[l], k_g[l], cos, sin)
        attn = _attn_call(q, k, v)
        lru = _lru_call(lx, glg, conv_w[l], conv_b[l], w_gate[l], lru_ba[l], lru_bx[l], lru_lambda[l])
        x = _merge_call(attn, lru, gates, x, mod[l], w_attn_o, w_lru_o, w_out, l)
        x = _ffn_call(x, mod[l], norm_g[l, 2], ffn2_up, ffn2_down, l, 6,
                      final_g=final_g if l == depth - 1 else None)
    return x
```
